```python
import math
import jax, jax.numpy as jnp
from jax import lax
import numpy as np

D_MODEL = 2048
BATCH = 4
SEQ = 4096
DEPTH = 2

Q_BLOCK = 128
HALF_WIDTH = D_MODEL // 2
HEAD_DIM_SB = 128
N_HEADS_SB = HALF_WIDTH // HEAD_DIM_SB
N_HEADS_DSA = 16
HEAD_DIM_DSA = HALF_WIDTH // N_HEADS_DSA
N_HEADS_IDX = 8
HEAD_DIM_IDX = 64
TOPK_MAX = 256
N_HEADS_DIL = 16
HEAD_DIM_DIL = D_MODEL // N_HEADS_DIL
DILATED_CONFIGS = ((128, 1), (512, 4), (2048, 16))
N_BUCKETS = 32
BUCKET_MAX_DIST = 2048
N_BIAS_HEADS = 16
D_FF = 4 * D_MODEL
N_EVEN = (DEPTH + 1) // 2
N_ODD = DEPTH // 2
DN_ALPHA = (2 * DEPTH) ** 0.25
DN_BETA = (8 * DEPTH) ** -0.25
LN_EPS = 1e-5
NEG = -1e30

EVEN_WIDTHS = (HALF_WIDTH, HALF_WIDTH, HALF_WIDTH,
               HALF_WIDTH, HALF_WIDTH, HALF_WIDTH,
               N_HEADS_IDX * HEAD_DIM_IDX, HEAD_DIM_IDX,
               N_HEADS_IDX)
EVEN_IN = sum(EVEN_WIDTHS)
ODD_IN = 3 * N_HEADS_DIL * HEAD_DIM_DIL

kernel_name = 'hybrid_stickbreak_dsa_dilated_deepnorm'


def layer_norm(x, g, b):
    xf = x.astype(jnp.float32)
    mu = jnp.mean(xf, axis=-1, keepdims=True)
    var = jnp.mean(jnp.square(xf - mu), axis=-1, keepdims=True)
    y = (xf - mu) * lax.rsqrt(var + LN_EPS) * g.astype(jnp.float32) + b.astype(jnp.float32)
    return y.astype(x.dtype)


def rel_bucket(dist):
    max_exact = N_BUCKETS // 2
    d_f = jnp.maximum(dist, 1).astype(jnp.float32)
    large = max_exact + (jnp.log(d_f / max_exact) / math.log(BUCKET_MAX_DIST / max_exact)
                         * (N_BUCKETS - max_exact)).astype(jnp.int32)
    large = jnp.minimum(large, N_BUCKETS - 1)
    return jnp.where(dist < max_exact, dist, large)


def stick_breaking_attention(q, k, v):
    B, S, H, dh = q.shape
    nb = S // Q_BLOCK
    qb = q.reshape(B, nb, Q_BLOCK, H, dh).transpose(1, 0, 2, 3, 4)
    s_pos = jnp.arange(S)
    scale = dh ** -0.5

    def block(args):
        qblk, n = args
        t_pos = n * Q_BLOCK + jnp.arange(Q_BLOCK)
        z = jnp.einsum('bqhe,bshe->bhqs', qblk, k).astype(jnp.float32) * scale
        strict = s_pos[None, :] < t_pos[:, None]
        log_beta = jax.nn.log_sigmoid(z)
        log_keep = jnp.where(strict, jax.nn.log_sigmoid(-z), 0.0)
        later = lax.cumsum(log_keep, axis=3, reverse=True) - log_keep
        a = jnp.where(strict, jnp.exp(log_beta + later), 0.0)
        return jnp.einsum('bhqs,bshe->bqhe', a, v.astype(jnp.float32)).astype(q.dtype)

    o = lax.map(block, (qb, jnp.arange(nb)))
    return o.transpose(1, 0, 2, 3, 4).reshape(B, S, H, dh)


def dsa_attention(q, k, v, q_idx, k_idx, w_idx, rel_bias):
    B, S, H, dh = q.shape
    nb = S // Q_BLOCK
    top_k = min(TOPK_MAX, S // 4)
    kv = jnp.concatenate([k, v], axis=-1)
    s_pos = jnp.arange(S)
    scale = dh ** -0.5
    idx_scale = HEAD_DIM_IDX ** -0.5
    w_scale = N_HEADS_IDX ** -0.5
    qb = q.reshape(B, nb, Q_BLOCK, H, dh).transpose(1, 0, 2, 3, 4)
    qib = q_idx.reshape(B, nb, Q_BLOCK, N_HEADS_IDX, HEAD_DIM_IDX).transpose(1, 0, 2, 3, 4)
    wb = w_idx.reshape(B, nb, Q_BLOCK, N_HEADS_IDX).transpose(1, 0, 2, 3)
    table = rel_bias.astype(jnp.float32)

    def block(args):
        qblk, qiblk, wblk, n = args
        t_pos = n * Q_BLOCK + jnp.arange(Q_BLOCK)
        dots = jnp.einsum('bqhe,bse->bqhs', qiblk, k_idx).astype(jnp.float32) * idx_scale
        score = jnp.einsum('bqhs,bqh->bqs', jax.nn.relu(dots), wblk.astype(jnp.float32) * w_scale)
        causal = s_pos[None, :] <= t_pos[:, None]
        score = jnp.where(causal[None], score, NEG)
        _, sel = lax.top_k(score, top_k)
        valid = sel <= t_pos[None, :, None]
        kv_sel = jax.vmap(lambda kvb, ib: kvb[ib])(kv, sel)
        k_sel, v_sel = jnp.split(kv_sel, 2, axis=-1)
        logits = jnp.einsum('bqhe,bqkhe->bhqk', qblk, k_sel).astype(jnp.float32) * scale
        dist = jnp.maximum(t_pos[None, :, None] - sel, 0)
        logits = logits + table[rel_bucket(dist)].transpose(0, 3, 1, 2)
        logits = jnp.where(valid[:, None], logits, NEG)
        p = jax.nn.softmax(logits, axis=-1)
        return jnp.einsum('bhqk,bqkhe->bqhe', p, v_sel.astype(jnp.float32)).astype(q.dtype)

    o = lax.map(block, (qb, qib, wb, jnp.arange(nb)))
    return o.transpose(1, 0, 2, 3, 4).reshape(B, S, H, dh)


def dilated_branch(q, k, v, rel_bias, window, dilation):
    B, S, H, dh = q.shape
    span = window // dilation
    chunk = dilation * Q_BLOCK
    Sp = -(-S // chunk) * chunk
    pad = Sp - S
    M = Sp // dilation
    nblk = M // Q_BLOCK

    def to_sub(a):
        a = jnp.pad(a, ((0, 0), (0, pad), (0, 0), (0, 0)))
        a = a.reshape(B, M, dilation, H, dh).transpose(0, 2, 1, 3, 4)
        return a.reshape(B, dilation, nblk, Q_BLOCK, H, dh)

    def with_prev(a):
        prev = jnp.pad(a, ((0, 0), (0, 0), (1, 0), (0, 0), (0, 0), (0, 0)))[:, :, :-1]
        return jnp.concatenate([prev, a], axis=3)

    qs = to_sub(q)
    kk = with_prev(to_sub(k))
    vv = with_prev(to_sub(v))
    logits = jnp.einsum('brnqhe,brnkhe->brnhqk', qs, kk).astype(jnp.float32) * (dh ** -0.5)
    i = jnp.arange(Q_BLOCK)[:, None]
    c = jnp.arange(2 * Q_BLOCK)[None, :]
    j = Q_BLOCK + i - c
    band = (j >= 0) & (j <= span)
    valid = band[None] & ((jnp.arange(nblk)[:, None, None] > 0) | (c >= Q_BLOCK)[None])
    bias = rel_bias.astype(jnp.float32)[rel_bucket(jnp.maximum(j, 0) * dilation)]
    logits = logits + bias.transpose(2, 0, 1)
    logits = jnp.where(valid[:, None], logits, NEG)
    m = jnp.max(logits, axis=-1, keepdims=True)
    e = jnp.exp(logits - m)
    s = jnp.sum(e, axis=-1)
    o = jnp.einsum('brnhqk,brnkhe->brnqhe', e, vv.astype(jnp.float32))
    o = o / s.transpose(0, 1, 2, 4, 3)[..., None]
    o = o.reshape(B, dilation, M, H, dh).transpose(0, 2, 1, 3, 4).reshape(B, Sp, H, dh)[:, :S]

    def stat_back(a):
        a = a.transpose(0, 1, 2, 4, 3).reshape(B, dilation, M, H)
        return a.transpose(0, 2, 1, 3).reshape(B, Sp, H)[:, :S]

    return o, stat_back(m[..., 0]), stat_back(s)


def dilated_attention(q, k, v, rel_bias):
    outs, maxes, dens = [], [], []
    for window, dilation in DILATED_CONFIGS:
        o, m, s = dilated_branch(q, k, v, rel_bias, window, dilation)
        outs.append(o)
        maxes.append(m)
        dens.append(s)
    o = jnp.stack(outs)
    m = jnp.stack(maxes)
    s = jnp.stack(dens)
    w = s * jnp.exp(m - jnp.max(m, axis=0, keepdims=True))
    out = jnp.sum(w[..., None] * o, axis=0) / jnp.sum(w, axis=0)[..., None]
    return out.astype(q.dtype)


def even_mixer(x, w_in, w_out, rel_bias):
    B, S, _ = x.shape
    h = x @ w_in
    offsets = [int(o) for o in np.cumsum(EVEN_WIDTHS)[:-1]]
    qa, ka, va, qb, kb, vb, qi, ki, wi = jnp.split(h, offsets, axis=-1)
    sb_shape = (B, S, N_HEADS_SB, HEAD_DIM_SB)
    dsa_shape = (B, S, N_HEADS_DSA, HEAD_DIM_DSA)
    oa = stick_breaking_attention(qa.reshape(sb_shape), ka.reshape(sb_shape), va.reshape(sb_shape))
    ob = dsa_attention(qb.reshape(dsa_shape), kb.reshape(dsa_shape), vb.reshape(dsa_shape),
                       qi.reshape(B, S, N_HEADS_IDX, HEAD_DIM_IDX), ki, wi, rel_bias)
    o = jnp.concatenate([oa.reshape(B, S, HALF_WIDTH), ob.reshape(B, S, HALF_WIDTH)], axis=-1)
    return o @ w_out


def odd_mixer(x, w_in, w_out, rel_bias):
    B, S, _ = x.shape
    q, k, v = jnp.split(x @ w_in, 3, axis=-1)
    shp = (B, S, N_HEADS_DIL, HEAD_DIM_DIL)
    o = dilated_attention(q.reshape(shp), k.reshape(shp), v.reshape(shp), rel_bias)
    return o.reshape(B, S, N_HEADS_DIL * HEAD_DIM_DIL) @ w_out


def sqrelu_mlp(x, w1, w2):
    return jnp.square(jax.nn.relu(x @ w1)) @ w2


def setup_inputs(seed: int = 0) -> dict:
    key = jax.random.key(seed)
    ks = jax.random.split(key, 12)
    f32 = jnp.float32
    even_scale = np.concatenate([
        np.ones(2 * HALF_WIDTH), np.full(HALF_WIDTH, DN_BETA),
        np.ones(2 * HALF_WIDTH), np.full(HALF_WIDTH, DN_BETA),
        np.ones(N_HEADS_IDX * HEAD_DIM_IDX + HEAD_DIM_IDX + N_HEADS_IDX)]).astype(np.float32)
    odd_scale = np.concatenate([np.ones(2 * D_MODEL), np.full(D_MODEL, DN_BETA)]).astype(np.float32)
    x = jax.random.normal(ks[0], (BATCH, SEQ, D_MODEL), f32)
    even_w_in = jax.random.normal(ks[1], (N_EVEN, D_MODEL, EVEN_IN), f32) * (D_MODEL ** -0.5) * jnp.asarray(even_scale)
    even_w_out = jax.random.normal(ks[2], (N_EVEN, 2 * HALF_WIDTH, D_MODEL), f32) * ((2 * HALF_WIDTH) ** -0.5 * DN_BETA)
    odd_w_in = jax.random.normal(ks[3], (N_ODD, D_MODEL, ODD_IN), f32) * (D_MODEL ** -0.5) * jnp.asarray(odd_scale)
    odd_w_out = jax.random.normal(ks[4], (N_ODD, D_MODEL, D_MODEL), f32) * (D_MODEL ** -0.5 * DN_BETA)
    rel_bias = jax.random.normal(ks[5], (N_BUCKETS, N_BIAS_HEADS), f32) * 0.5
    ln_mix_g = 1.0 + 0.05 * jax.random.normal(ks[6], (DEPTH, D_MODEL), f32)
    ln_mix_b = 0.02 * jax.random.normal(ks[7], (DEPTH, D_MODEL), f32)
    ffn_w1 = jax.random.normal(ks[8], (DEPTH, D_MODEL, D_FF), f32) * (D_MODEL ** -0.5 * DN_BETA)
    ffn_w2 = jax.random.normal(ks[9], (DEPTH, D_FF, D_MODEL), f32) * (D_FF ** -0.5 * DN_BETA)
    ln_ffn_g = 1.0 + 0.05 * jax.random.normal(ks[10], (DEPTH, D_MODEL), f32)
    ln_ffn_b = 0.02 * jax.random.normal(ks[11], (DEPTH, D_MODEL), f32)
    return {'x': x, 'even_w_in': even_w_in, 'even_w_out': even_w_out,
            'odd_w_in': odd_w_in, 'odd_w_out': odd_w_out, 'rel_bias': rel_bias,
            'ln_mix_g': ln_mix_g, 'ln_mix_b': ln_mix_b, 'ffn_w1': ffn_w1, 'ffn_w2': ffn_w2,
            'ln_ffn_g': ln_ffn_g, 'ln_ffn_b': ln_ffn_b}


def reference(x, even_w_in, even_w_out, odd_w_in, odd_w_out, rel_bias,
              ln_mix_g, ln_mix_b, ffn_w1, ffn_w2, ln_ffn_g, ln_ffn_b):
    for layer in range(DEPTH):
        if layer % 2 == 0:
            mix = even_mixer(x, even_w_in[layer // 2], even_w_out[layer // 2], rel_bias)
        else:
            mix = odd_mixer(x, odd_w_in[layer // 2], odd_w_out[layer // 2], rel_bias)
        x = layer_norm(DN_ALPHA * x + mix, ln_mix_g[layer], ln_mix_b[layer])
        x = layer_norm(DN_ALPHA * x + sqrelu_mlp(x, ffn_w1[layer], ffn_w2[layer]),
                       ln_ffn_g[layer], ln_ffn_b[layer])
    return x
```

```python
import functools
import math

import jax
import jax.numpy as jnp
from jax import lax
from jax.experimental import pallas as pl
from jax.experimental.pallas import tpu as pltpu

F32 = jnp.float32
BF16 = jnp.bfloat16
I32 = jnp.int32

Q_BLOCK = 128
HEAD_DIM_SB = 128
N_HEADS_SB = 8
N_HEADS_DSA = 16
HEAD_DIM_DSA = 64
N_HEADS_IDX = 8
HEAD_DIM_IDX = 64
TOPK_MAX = 256
N_HEADS_DIL = 16
HEAD_DIM_DIL = 128
DILATED_CONFIGS = ((128, 1), (512, 4), (2048, 16))
N_BUCKETS = 32
BUCKET_MAX_DIST = 2048
DEPTH = 2
DN_ALPHA = (2 * DEPTH) ** 0.25
LN_EPS = 1e-5
NEG = -1e30
INT_MIN = -(2 ** 31)

LANES = 128
VMEM_LIMIT_BYTES = 52 * 1024 * 1024


def _cparams(sem):
    return pltpu.CompilerParams(dimension_semantics=sem, vmem_limit_bytes=VMEM_LIMIT_BYTES)


def _dot(a, b):
    return jnp.dot(a, b, preferred_element_type=F32)


def _dot_nt(a, b):
    return lax.dot_general(a, b, (((1,), (1,)), ((), ())), preferred_element_type=F32)


def _split2(x):
    hi = x.astype(BF16)
    lo = (x - hi.astype(F32)).astype(BF16)
    return hi, lo


def _mm_kernel(a_ref, b_ref, o_ref):
    o_ref[...] = _dot(a_ref[...].astype(BF16), b_ref[...]).astype(o_ref.dtype)


def matmul(a, b, out_dtype, tm=1024, tn=1024):
    M, K = a.shape
    N = b.shape[1]
    tm, tn = min(tm, M), min(tn, N)
    assert M % tm == 0 and N % tn == 0
    return pl.pallas_call(
        _mm_kernel,
        grid=(M // tm, N // tn),
        in_specs=[pl.BlockSpec((tm, K), lambda i, j: (i, 0)),
                  pl.BlockSpec((K, tn), lambda i, j: (0, j))],
        out_specs=pl.BlockSpec((tm, tn), lambda i, j: (i, j)),
        out_shape=jax.ShapeDtypeStruct((M, N), out_dtype),
        compiler_params=_cparams(("parallel", "arbitrary")),
        name="proj_matmul",
    )(a, b)


def _mm3_kernel(a_ref, bh_ref, bl_ref, o_ref):
    ah, al = _split2(a_ref[...])
    bh = bh_ref[...]
    o_ref[...] = _dot(ah, bh) + (_dot(ah, bl_ref[...]) + _dot(al, bh))


def matmul_split(a, b, tm=512):
    M, K = a.shape
    N = b.shape[1]
    tm = min(tm, M)
    assert M % tm == 0
    bh, bl = _split2(b)
    return pl.pallas_call(
        _mm3_kernel,
        grid=(M // tm,),
        in_specs=[pl.BlockSpec((tm, K), lambda i: (i, 0)),
                  pl.BlockSpec((K, N), lambda i: (0, 0)),
                  pl.BlockSpec((K, N), lambda i: (0, 0))],
        out_specs=pl.BlockSpec((tm, N), lambda i: (i, 0)),
        out_shape=jax.ShapeDtypeStruct((M, N), F32),
        compiler_params=_cparams(("parallel",)),
        name="indexer_proj",
    )(a, bh, bl)


def _layer_norm(y, g, b):
    mu = jnp.mean(y, axis=-1, keepdims=True)
    yc = y - mu
    var = jnp.mean(jnp.square(yc), axis=-1, keepdims=True)
    return yc * lax.rsqrt(var + LN_EPS) * g + b


def _proj_ln_kernel(*refs, n_in):
    a_refs, w_refs = refs[:n_in], refs[n_in:2 * n_in]
    x_ref, g_ref, b_ref, y_ref, yb_ref = refs[2 * n_in:]
    acc = _dot(a_refs[0][...], w_refs[0][...])
    for a_ref, w_ref in zip(a_refs[1:], w_refs[1:]):
        acc = acc + _dot(a_ref[...], w_ref[...])
    y = _layer_norm(DN_ALPHA * x_ref[...] + acc, g_ref[...], b_ref[...])
    y_ref[...] = y
    yb_ref[...] = y.astype(BF16)


def proj_residual_ln(a_list, w_list, x, g, b, tm=512):
    M, D = x.shape
    tm = min(tm, M)
    assert M % tm == 0
    n_in = len(a_list)
    in_specs = ([pl.BlockSpec((tm, a.shape[1]), lambda i: (i, 0)) for a in a_list]
                + [pl.BlockSpec(w.shape, lambda i: (0, 0)) for w in w_list]
                + [pl.BlockSpec((tm, D), lambda i: (i, 0)),
                   pl.BlockSpec((1, D), lambda i: (0, 0)),
                   pl.BlockSpec((1, D), lambda i: (0, 0))])
    return pl.pallas_call(
        functools.partial(_proj_ln_kernel, n_in=n_in),
        grid=(M // tm,),
        in_specs=in_specs,
        out_specs=[pl.BlockSpec((tm, D), lambda i: (i, 0)),
                   pl.BlockSpec((tm, D), lambda i: (i, 0))],
        out_shape=[jax.ShapeDtypeStruct((M, D), F32), jax.ShapeDtypeStruct((M, D), BF16)],
        compiler_params=_cparams(("parallel",)),
        name="out_proj_ln",
    )(*a_list, *w_list, x, g.reshape(1, D), b.reshape(1, D))


def _ffn_kernel(xb_ref, x_ref, w1_ref, w2_ref, g_ref, b_ref, y_ref, yb_ref, acc_ref):
    f = pl.program_id(1)

    @pl.when(f == 0)
    def _():
        acc_ref[...] = jnp.zeros_like(acc_ref)

    h = _dot(xb_ref[...], w1_ref[...])
    h = jnp.square(jnp.maximum(h, 0.0)).astype(BF16)
    acc_ref[...] += _dot(h, w2_ref[...])

    @pl.when(f == pl.num_programs(1) - 1)
    def _():
        y = _layer_norm(DN_ALPHA * x_ref[...] + acc_ref[...], g_ref[...], b_ref[...])
        y_ref[...] = y
        yb_ref[...] = y.astype(BF16)


def ffn_residual_ln(x, xb, w1, w2, g, b, tm=512, tf=512):
    M, D = x.shape
    F = w1.shape[1]
    tm, tf = min(tm, M), min(tf, F)
    assert M % tm == 0 and F % tf == 0
    return pl.pallas_call(
        _ffn_kernel,
        grid=(M // tm, F // tf),
        in_specs=[pl.BlockSpec((tm, D), lambda i, f: (i, 0)),
                  pl.BlockSpec((tm, D), lambda i, f: (i, 0)),
                  pl.BlockSpec((D, tf), lambda i, f: (0, f)),
                  pl.BlockSpec((tf, D), lambda i, f: (f, 0)),
                  pl.BlockSpec((1, D), lambda i, f: (0, 0)),
                  pl.BlockSpec((1, D), lambda i, f: (0, 0))],
        out_specs=[pl.BlockSpec((tm, D), lambda i, f: (i, 0)),
                   pl.BlockSpec((tm, D), lambda i, f: (i, 0))],
        out_shape=[jax.ShapeDtypeStruct((M, D), F32), jax.ShapeDtypeStruct((M, D), BF16)],
        scratch_shapes=[pltpu.VMEM((tm, D), F32)],
        compiler_params=_cparams(("parallel", "arbitrary")),
        name="ffn_ln",
    )(xb, x, w1, w2, g.reshape(1, D), b.reshape(1, D))


def _sb_kernel(q_ref, k_ref, v_ref, o_ref, *, blk, scale):
    n = pl.program_id(2)
    q = q_ref[...]
    row = lax.broadcasted_iota(I32, (blk, blk), 0)
    col = lax.broadcasted_iota(I32, (blk, blk), 1)
    tri = jnp.where(row > col, 1.0, 0.0).astype(BF16)
    strict = col < row

    def tile(kb, carry, acc, diag):
        start = pl.multiple_of(kb * blk, blk)
        k = k_ref[pl.ds(start, blk), :]
        v = v_ref[pl.ds(start, blk), :]
        z = _dot_nt(q, k) * scale
        softplus_neg_abs = jnp.log1p(jnp.exp(-jnp.abs(z)))
        log_beta = jnp.minimum(z, 0.0) - softplus_neg_abs
        log_keep = log_beta - z
        if diag:
            log_keep = jnp.where(strict, log_keep, 0.0)
        hi = log_keep.astype(BF16)
        r1 = log_keep - hi.astype(F32)
        mid = r1.astype(BF16)
        lo = (r1 - mid.astype(F32)).astype(BF16)
        later = _dot(hi, tri) + (_dot(mid, tri) + _dot(lo, tri)) + carry
        a = jnp.exp(log_beta + later)
        if diag:
            a = jnp.where(strict, a, 0.0)
        acc = acc + _dot(a.astype(BF16), v)
        carry = carry + jnp.sum(log_keep, axis=1, keepdims=True)
        return carry, acc

    carry = jnp.zeros((blk, 1), F32)
    acc = jnp.zeros((blk, q.shape[1]), F32)
    carry, acc = tile(n, carry, acc, True)
    carry, acc = lax.fori_loop(0, n, lambda it, c: tile(n - 1 - it, c[0], c[1], False), (carry, acc))
    o_ref[...] = acc.astype(o_ref.dtype)


def stick_breaking(h, B, S, q_col, k_col, v_col, n_heads):
    blk = Q_BLOCK
    dh = HEAD_DIM_SB
    nq = S // blk
    return pl.pallas_call(
        functools.partial(_sb_kernel, blk=blk, scale=dh ** -0.5),
        grid=(B, n_heads, nq),
        in_specs=[pl.BlockSpec((None, blk, dh), lambda b, hh, i: (b, i, q_col + hh)),
                  pl.BlockSpec((None, S, dh), lambda b, hh, i: (b, 0, k_col + hh)),
                  pl.BlockSpec((None, S, dh), lambda b, hh, i: (b, 0, v_col + hh))],
        out_specs=pl.BlockSpec((None, blk, dh), lambda b, hh, i: (b, i, hh)),
        out_shape=jax.ShapeDtypeStruct((B, S, n_heads * dh), BF16),
        compiler_params=_cparams(("parallel", "parallel", "arbitrary")),
        name="stick_breaking",
    )(h, h, h)


def _sel_kernel(qi_ref, ki_ref, wi_ref, o_ref, keys_ref, *, blk, nk, topk, idx_scale, w_scale):
    n = pl.program_id(1)
    nh = qi_ref.shape[0]
    qall = qi_ref[...].reshape(nh * blk, qi_ref.shape[2]) * idx_scale
    qh, ql = _split2(qall)
    w8 = wi_ref[...] * w_scale
    row = lax.broadcasted_iota(I32, (blk, blk), 0)
    col = lax.broadcasted_iota(I32, (blk, blk), 1)
    causal = row <= col
    tri = jnp.where(col < row, 1.0, 0.0).astype(BF16)

    def score_keys(kb, diag):
        start = pl.multiple_of(kb * blk, blk)
        kh, kl = _split2(ki_ref[pl.ds(start, blk), :])
        d = _dot_nt(kh, qh) + (_dot_nt(kh, ql) + _dot_nt(kl, qh))
        d = jnp.maximum(d, 0.0)
        sc = d[:, 0:blk] * w8[0:1, :]
        for hh in range(1, nh):
            sc = sc + d[:, hh * blk:(hh + 1) * blk] * w8[hh:hh + 1, :]
        sc = jnp.where(sc == 0.0, 0.0, sc)
        bits = pltpu.bitcast(sc, I32)
        key = jnp.where(bits < 0, bits ^ 0x7FFFFFFF, bits)
        if diag:
            key = jnp.where(causal, key, INT_MIN)
        keys_ref[kb] = key

    score_keys(n, True)
    lax.fori_loop(0, n, lambda kb, c: (score_keys(kb, False), c)[1], 0)

    def count(pred):
        def body(kb, c):
            return c + jnp.where(pred(keys_ref[kb]), 1, 0)
        c = lax.fori_loop(0, n + 1, body, jnp.zeros((blk, blk), I32))
        return jnp.sum(c, axis=0, keepdims=True)

    c0 = count(lambda key: key >= 0)
    theta = jnp.where(c0 >= topk, 0, INT_MIN).astype(I32)

    def bit_step(i, theta):
        cand = theta + lax.shift_left(jnp.int32(1), 30 - i)
        c = count(lambda key: key >= cand)
        return jnp.where(c >= topk, cand, theta)

    theta = lax.fori_loop(0, 31, bit_step, theta)
    need = (topk - count(lambda key: key > theta)).astype(F32)

    def emit(kb, carry, diag):
        key = keys_ref[kb]
        eq = key == theta
        eqf = jnp.where(eq, 1.0, 0.0)
        rank = _dot(tri, eqf.astype(BF16)) + carry
        sel = jnp.where(key > theta, 1.0, jnp.where(rank < need, eqf, 0.0))
        if diag:
            sel = jnp.where(causal, sel, 0.0)
        madd = jnp.where(sel > 0.5, 0.0, -jnp.inf)
        o_ref[kb] = madd.T.astype(o_ref.dtype)
        return carry + jnp.sum(eqf, axis=0, keepdims=True)

    carry = lax.fori_loop(0, n, lambda kb, c: emit(kb, c, False), jnp.zeros((1, blk), F32))
    emit(n, carry, True)

    def fill(kb, c):
        o_ref[kb] = jnp.full((blk, blk), -jnp.inf, o_ref.dtype)
        return c

    lax.fori_loop(n + 1, nk, fill, 0)


def dsa_select(qi, ki, wi, topk):
    B, nh, S, di = qi.shape
    blk = Q_BLOCK
    nq = S // blk
    return pl.pallas_call(
        functools.partial(_sel_kernel, blk=blk, nk=nq, topk=topk,
                          idx_scale=HEAD_DIM_IDX ** -0.5, w_scale=N_HEADS_IDX ** -0.5),
        grid=(B, nq),
        in_specs=[pl.BlockSpec((None, nh, blk, di), lambda b, i: (b, 0, i, 0)),
                  pl.BlockSpec((None, S, di), lambda b, i: (b, 0, 0)),
                  pl.BlockSpec((None, nh, blk), lambda b, i: (b, 0, i))],
        out_specs=pl.BlockSpec((None, None, nq, blk, blk), lambda b, i: (b, i, 0, 0, 0)),
        out_shape=jax.ShapeDtypeStruct((B, nq, nq, blk, blk), BF16),
        scratch_shapes=[pltpu.VMEM((nq, blk, blk), I32)],
        compiler_params=_cparams(("parallel", "arbitrary")),
        name="dsa_select",
    )(qi, ki, wi)


def _dsa_kernel(q_ref, k_ref, v_ref, m_ref, bias_ref, o_ref, *, blk, dh):
    n = pl.program_id(2)
    q = q_ref[...] * (dh ** -0.5)
    lane = lax.broadcasted_iota(I32, q.shape, 1)
    halves = (lane < dh, lane >= dh)
    qs = [jnp.where(hm, q, jnp.zeros_like(q)) for hm in halves]

    def body(kb, carry):
        start = pl.multiple_of(kb * blk, blk)
        k = k_ref[pl.ds(start, blk), :]
        v = v_ref[pl.ds(start, blk), :]
        madd = m_ref[kb].astype(F32)
        out = []
        for hh in range(2):
            m, l, acc = carry[hh]
            s = _dot_nt(qs[hh], k) + bias_ref[hh, n - kb] + madd
            m_new = jnp.maximum(m, jnp.max(s, axis=1, keepdims=True))
            alpha = jnp.exp(m - m_new)
            p = jnp.exp(s - m_new)
            l = l * alpha + jnp.sum(p, axis=1, keepdims=True)
            acc = acc * alpha + _dot(p.astype(BF16), v)
            out.append((m_new, l, acc))
        return tuple(out)

    init = tuple((jnp.full((blk, 1), NEG, F32), jnp.zeros((blk, 1), F32),
                  jnp.zeros((blk, LANES), F32)) for _ in range(2))
    (_, l0, a0), (_, l1, a1) = lax.fori_loop(0, n + 1, body, init)
    o_ref[...] = jnp.where(halves[0], a0 / l0, a1 / l1).astype(o_ref.dtype)


def dsa_attention(h, madd, bias_tiles, B, S, q_col, k_col, v_col):
    blk = Q_BLOCK
    nq = S // blk
    n_pairs = N_HEADS_DSA * HEAD_DIM_DSA // LANES
    return pl.pallas_call(
        functools.partial(_dsa_kernel, blk=blk, dh=HEAD_DIM_DSA),
        grid=(B, n_pairs, nq),
        in_specs=[pl.BlockSpec((None, blk, LANES), lambda b, p, i: (b, i, q_col + p)),
                  pl.BlockSpec((None, S, LANES), lambda b, p, i: (b, 0, k_col + p)),
                  pl.BlockSpec((None, S, LANES), lambda b, p, i: (b, 0, v_col + p)),
                  pl.BlockSpec((None, None, nq, blk, blk), lambda b, p, i: (b, i, 0, 0, 0)),
                  pl.BlockSpec((2, nq, blk, blk), lambda b, p, i: (p, 0, 0, 0))],
        out_specs=pl.BlockSpec((None, blk, LANES), lambda b, p, i: (b, i, p)),
        out_shape=jax.ShapeDtypeStruct((B, S, n_pairs * LANES), BF16),
        compiler_params=_cparams(("parallel", "parallel", "arbitrary")),
        name="dsa_attention",
    )(h, h, h, madd, bias_tiles)


def _dil_kernel(q_ref, k_ref, v_ref, bias_ref, o_ref, qf, kf, vf, m_s, l_s, acc_s,
                *, blk, chunk, dils, scale):
    c = pl.program_id(2)
    base = pl.multiple_of(c * chunk, chunk)

    @pl.when(c == 0)
    def _():
        qf[...] = q_ref[...].astype(F32)
        kf[...] = k_ref[...].astype(F32)
        vf[...] = v_ref[...].astype(F32)

    row = lax.broadcasted_iota(I32, (blk, 2 * blk), 0)
    colw = lax.broadcasted_iota(I32, (blk, 2 * blk), 1)
    dist = blk + row - colw
    band = (dist >= 0) & (dist <= blk)

    for g, d in enumerate(dils):
        span = blk * d
        tiles = chunk // blk

        def body(t, carry, g=g, d=d, span=span):
            u = t // d
            r = t - u * d
            start = base + u * span + r
            has_prev = start >= span
            prev = jnp.where(has_prev, start - span, start)
            if d == 1:
                start = pl.multiple_of(start, blk)
                prev = pl.multiple_of(prev, blk)
                rows, prows = pl.ds(start, blk), pl.ds(prev, blk)
                lrows = pl.ds(pl.multiple_of(start - base, blk), blk)
            else:
                rows, prows = pl.ds(start, blk, stride=d), pl.ds(prev, blk, stride=d)
                lrows = pl.ds(start - base, blk, stride=d)
            q = qf[rows, :].astype(BF16)
            kk = jnp.concatenate([kf[prows, :], kf[rows, :]], axis=0).astype(BF16)
            vv = jnp.concatenate([vf[prows, :], vf[rows, :]], axis=0).astype(BF16)
            s = _dot_nt(q, kk) * scale + bias_ref[g]
            first_col = jnp.where(has_prev, 0, blk)
            s = jnp.where(band & (colw >= first_col), s, NEG)
            m_t = jnp.max(s, axis=1, keepdims=True)
            e = jnp.exp(s - m_t)
            l_t = jnp.sum(e, axis=1, keepdims=True)
            o_t = _dot(e.astype(BF16), vv)
            if g == 0:
                m_s[lrows, :] = jnp.broadcast_to(m_t, (blk, LANES))
                l_s[lrows, :] = jnp.broadcast_to(l_t, (blk, LANES))
                acc_s[lrows, :] = o_t
            else:
                m_old = m_s[lrows, :]
                m_new = jnp.maximum(m_old, m_t)
                a_old = jnp.exp(m_old - m_new)
                a_t = jnp.exp(m_t - m_new)
                m_s[lrows, :] = m_new
                l_s[lrows, :] = l_s[lrows, :] * a_old + l_t * a_t
                acc_s[lrows, :] = acc_s[lrows, :] * a_old + o_t * a_t
            return carry

        lax.fori_loop(0, tiles, body, 0)

    o_ref[...] = (acc_s[...] / l_s[...]).astype(o_ref.dtype)


def dilated_attention(h, bias_tiles, B, S, q_col, k_col, v_col, n_heads):
    blk = Q_BLOCK
    dils = tuple(d for _, d in DILATED_CONFIGS)
    assert all(w // d == blk for w, d in DILATED_CONFIGS)
    chunk = blk * max(dils)
    assert S % chunk == 0
    dh = HEAD_DIM_DIL
    return pl.pallas_call(
        functools.partial(_dil_kernel, blk=blk, chunk=chunk, dils=dils, scale=dh ** -0.5),
        grid=(B, n_heads, S // chunk),
        in_specs=[pl.BlockSpec((None, S, dh), lambda b, hh, c: (b, 0, q_col + hh)),
                  pl.BlockSpec((None, S, dh), lambda b, hh, c: (b, 0, k_col + hh)),
                  pl.BlockSpec((None, S, dh), lambda b, hh, c: (b, 0, v_col + hh)),
                  pl.BlockSpec((None, len(dils), blk, 2 * blk), lambda b, hh, c: (hh, 0, 0, 0))],
        out_specs=pl.BlockSpec((None, chunk, dh), lambda b, hh, c: (b, c, hh)),
        out_shape=jax.ShapeDtypeStruct((B, S, n_heads * dh), BF16),
        scratch_shapes=[pltpu.VMEM((S, dh), F32), pltpu.VMEM((S, dh), F32), pltpu.VMEM((S, dh), F32),
                        pltpu.VMEM((chunk, LANES), F32), pltpu.VMEM((chunk, LANES), F32),
                        pltpu.VMEM((chunk, dh), F32)],
        compiler_params=_cparams(("parallel", "parallel", "arbitrary")),
        name="dilated_attention",
    )(h, h, h, bias_tiles)


def _rel_bucket(dist):
    max_exact = N_BUCKETS // 2
    d_f = jnp.maximum(dist, 1).astype(F32)
    large = max_exact + (jnp.log(d_f / max_exact) / math.log(BUCKET_MAX_DIST / max_exact)
                         * (N_BUCKETS - max_exact)).astype(I32)
    large = jnp.minimum(large, N_BUCKETS - 1)
    return jnp.where(dist < max_exact, dist, large)


def _causal_bias_tiles(rel_bias, nq):
    blk = Q_BLOCK
    i = jnp.arange(blk, dtype=I32)[:, None]
    j = jnp.arange(blk, dtype=I32)[None, :]
    bd = jnp.arange(nq, dtype=I32)[:, None, None]
    dist = jnp.maximum(bd * blk + i - j, 0)
    return jnp.transpose(rel_bias.astype(F32)[_rel_bucket(dist)], (3, 0, 1, 2))


def _dilated_bias_tiles(rel_bias):
    blk = Q_BLOCK
    i = jnp.arange(blk, dtype=I32)[:, None]
    c = jnp.arange(2 * blk, dtype=I32)[None, :]
    j = jnp.maximum(blk + i - c, 0)
    tiles = [rel_bias.astype(F32)[_rel_bucket(j * d)] for _, d in DILATED_CONFIGS]
    return jnp.transpose(jnp.stack(tiles), (3, 0, 1, 2))


def even_mixer(x2, xb, w_in, w_out, rel_bias, B, S):
    half = N_HEADS_SB * HEAD_DIM_SB
    n_main = 6 * half
    n_qi = N_HEADS_IDX * HEAD_DIM_IDX
    slab = half // LANES
    h = matmul(xb, w_in[:, :n_main].astype(BF16), BF16).reshape(B, S, n_main)
    w_idx = w_in[:, n_main:]
    pad = -w_idx.shape[1] % LANES
    hi = matmul_split(x2, jnp.pad(w_idx, ((0, 0), (0, pad))))
    qi = hi[:, :n_qi].reshape(B, S, N_HEADS_IDX, HEAD_DIM_IDX).transpose(0, 2, 1, 3)
    ki = hi[:, n_qi:n_qi + HEAD_DIM_IDX].reshape(B, S, HEAD_DIM_IDX)
    wi = hi[:, n_qi + HEAD_DIM_IDX:n_qi + HEAD_DIM_IDX + N_HEADS_IDX].reshape(B, S, N_HEADS_IDX)
    wi = wi.transpose(0, 2, 1)
    madd = dsa_select(qi, ki, wi, min(TOPK_MAX, S // 4))
    oa = stick_breaking(h, B, S, 0, slab, 2 * slab, N_HEADS_SB)
    ob = dsa_attention(h, madd, _causal_bias_tiles(rel_bias, S // Q_BLOCK), B, S,
                       3 * slab, 4 * slab, 5 * slab)
    w_out = w_out.astype(BF16)
    return [oa.reshape(B * S, half), ob.reshape(B * S, half)], [w_out[:half], w_out[half:]]


def odd_mixer(xb, w_in, w_out, rel_bias, B, S):
    width = N_HEADS_DIL * HEAD_DIM_DIL
    h = matmul(xb, w_in.astype(BF16), BF16).reshape(B, S, 3 * width)
    o = dilated_attention(h, _dilated_bias_tiles(rel_bias), B, S, 0, N_HEADS_DIL, 2 * N_HEADS_DIL,
                          N_HEADS_DIL)
    return [o.reshape(B * S, width)], [w_out.astype(BF16)]


def kernel(x, even_w_in, even_w_out, odd_w_in, odd_w_out, rel_bias, ln_mix_g, ln_mix_b,
           ffn_w1, ffn_w2, ln_ffn_g, ln_ffn_b):
    B, S, D = x.shape
    x2 = x.reshape(B * S, D)
    xb = x2.astype(BF16)
    for layer in range(DEPTH):
        if layer % 2 == 0:
            a_list, w_list = even_mixer(x2, xb, even_w_in[layer // 2], even_w_out[layer // 2],
                                        rel_bias, B, S)
        else:
            a_list, w_list = odd_mixer(xb, odd_w_in[layer // 2], odd_w_out[layer // 2],
                                       rel_bias, B, S)
        x2, xb = proj_residual_ln(a_list, w_list, x2, ln_mix_g[layer], ln_mix_b[layer])
        x2, xb = ffn_residual_ln(x2, xb, ffn_w1[layer].astype(BF16), ffn_w2[layer].astype(BF16),
                                 ln_ffn_g[layer], ln_ffn_b[layer])
    return x2.reshape(B, S, D)
```

```python
import functools
import math

import jax
import jax.numpy as jnp
from jax import lax
from jax.experimental import pallas as pl
from jax.experimental.pallas import tpu as pltpu

F32 = jnp.float32
BF16 = jnp.bfloat16
I32 = jnp.int32

Q_BLOCK = 128
HEAD_DIM_SB = 128
N_HEADS_SB = 8
N_HEADS_DSA = 16
HEAD_DIM_DSA = 64
N_HEADS_IDX = 8
HEAD_DIM_IDX = 64
TOPK_MAX = 256
N_HEADS_DIL = 16
HEAD_DIM_DIL = 128
DILATED_CONFIGS = ((128, 1), (512, 4), (2048, 16))
N_BUCKETS = 32
BUCKET_MAX_DIST = 2048
DEPTH = 2
DN_ALPHA = (2 * DEPTH) ** 0.25
LN_EPS = 1e-5
NEG = -1e30
INT_MIN = -(2 ** 31)

LANES = 128
VMEM_LIMIT_BYTES = 52 * 1024 * 1024


def _cparams(sem):
    return pltpu.CompilerParams(dimension_semantics=sem, vmem_limit_bytes=VMEM_LIMIT_BYTES)


def _dot(a, b):
    return jnp.dot(a, b, preferred_element_type=F32)


def _dot_nt(a, b):
    return lax.dot_general(a, b, (((1,), (1,)), ((), ())), preferred_element_type=F32)


def _split2(x):
    hi = x.astype(BF16)
    lo = (x - hi.astype(F32)).astype(BF16)
    return hi, lo


def _mm_kernel(a_ref, b_ref, o_ref):
    o_ref[...] = _dot(a_ref[...].astype(BF16), b_ref[...]).astype(o_ref.dtype)


def matmul(a, b, out_dtype, tm=1024, tn=1024):
    M, K = a.shape
    N = b.shape[1]
    tm, tn = min(tm, M), min(tn, N)
    assert M % tm == 0 and N % tn == 0
    return pl.pallas_call(
        _mm_kernel,
        grid=(M // tm, N // tn),
        in_specs=[pl.BlockSpec((tm, K), lambda i, j: (i, 0)),
                  pl.BlockSpec((K, tn), lambda i, j: (0, j))],
        out_specs=pl.BlockSpec((tm, tn), lambda i, j: (i, j)),
        out_shape=jax.ShapeDtypeStruct((M, N), out_dtype),
        compiler_params=_cparams(("parallel", "arbitrary")),
        name="proj_matmul",
    )(a, b)


def _mm3_kernel(a_ref, bh_ref, bl_ref, o_ref):
    ah, al = _split2(a_ref[...])
    bh = bh_ref[...]
    o_ref[...] = _dot(ah, bh) + (_dot(ah, bl_ref[...]) + _dot(al, bh))


def matmul_split(a, b, tm=512):
    M, K = a.shape
    N = b.shape[1]
    tm = min(tm, M)
    assert M % tm == 0
    bh, bl = _split2(b)
    return pl.pallas_call(
        _mm3_kernel,
        grid=(M // tm,),
        in_specs=[pl.BlockSpec((tm, K), lambda i: (i, 0)),
                  pl.BlockSpec((K, N), lambda i: (0, 0)),
                  pl.BlockSpec((K, N), lambda i: (0, 0))],
        out_specs=pl.BlockSpec((tm, N), lambda i: (i, 0)),
        out_shape=jax.ShapeDtypeStruct((M, N), F32),
        compiler_params=_cparams(("parallel",)),
        name="indexer_proj",
    )(a, bh, bl)


def _layer_norm(y, g, b):
    mu = jnp.mean(y, axis=-1, keepdims=True)
    yc = y - mu
    var = jnp.mean(jnp.square(yc), axis=-1, keepdims=True)
    return yc * lax.rsqrt(var + LN_EPS) * g + b


def _proj_ln_kernel(*refs, n_in):
    a_refs, w_refs = refs[:n_in], refs[n_in:2 * n_in]
    x_ref, g_ref, b_ref, y_ref, yb_ref = refs[2 * n_in:]
    acc = _dot(a_refs[0][...], w_refs[0][...])
    for a_ref, w_ref in zip(a_refs[1:], w_refs[1:]):
        acc = acc + _dot(a_ref[...], w_ref[...])
    y = _layer_norm(DN_ALPHA * x_ref[...] + acc, g_ref[...], b_ref[...])
    y_ref[...] = y
    yb_ref[...] = y.astype(BF16)


def proj_residual_ln(a_list, w_list, x, g, b, tm=512):
    M, D = x.shape
    tm = min(tm, M)
    assert M % tm == 0
    n_in = len(a_list)
    in_specs = ([pl.BlockSpec((tm, a.shape[1]), lambda i: (i, 0)) for a in a_list]
                + [pl.BlockSpec(w.shape, lambda i: (0, 0)) for w in w_list]
                + [pl.BlockSpec((tm, D), lambda i: (i, 0)),
                   pl.BlockSpec((1, D), lambda i: (0, 0)),
                   pl.BlockSpec((1, D), lambda i: (0, 0))])
    return pl.pallas_call(
        functools.partial(_proj_ln_kernel, n_in=n_in),
        grid=(M // tm,),
        in_specs=in_specs,
        out_specs=[pl.BlockSpec((tm, D), lambda i: (i, 0)),
                   pl.BlockSpec((tm, D), lambda i: (i, 0))],
        out_shape=[jax.ShapeDtypeStruct((M, D), F32), jax.ShapeDtypeStruct((M, D), BF16)],
        compiler_params=_cparams(("parallel",)),
        name="out_proj_ln",
    )(*a_list, *w_list, x, g.reshape(1, D), b.reshape(1, D))


def _ffn_kernel(xb_ref, x_ref, w1_ref, w2_ref, g_ref, b_ref, y_ref, yb_ref, acc_ref):
    f = pl.program_id(1)

    @pl.when(f == 0)
    def _():
        acc_ref[...] = jnp.zeros_like(acc_ref)

    h = _dot(xb_ref[...], w1_ref[...])
    h = jnp.square(jnp.maximum(h, 0.0)).astype(BF16)
    acc_ref[...] += _dot(h, w2_ref[...])

    @pl.when(f == pl.num_programs(1) - 1)
    def _():
        y = _layer_norm(DN_ALPHA * x_ref[...] + acc_ref[...], g_ref[...], b_ref[...])
        y_ref[...] = y
        yb_ref[...] = y.astype(BF16)


def ffn_residual_ln(x, xb, w1, w2, g, b, tm=512, tf=512):
    M, D = x.shape
    F = w1.shape[1]
    tm, tf = min(tm, M), min(tf, F)
    assert M % tm == 0 and F % tf == 0
    return pl.pallas_call(
        _ffn_kernel,
        grid=(M // tm, F // tf),
        in_specs=[pl.BlockSpec((tm, D), lambda i, f: (i, 0)),
                  pl.BlockSpec((tm, D), lambda i, f: (i, 0)),
                  pl.BlockSpec((D, tf), lambda i, f: (0, f)),
                  pl.BlockSpec((tf, D), lambda i, f: (f, 0)),
                  pl.BlockSpec((1, D), lambda i, f: (0, 0)),
                  pl.BlockSpec((1, D), lambda i, f: (0, 0))],
        out_specs=[pl.BlockSpec((tm, D), lambda i, f: (i, 0)),
                   pl.BlockSpec((tm, D), lambda i, f: (i, 0))],
        out_shape=[jax.ShapeDtypeStruct((M, D), F32), jax.ShapeDtypeStruct((M, D), BF16)],
        scratch_shapes=[pltpu.VMEM((tm, D), F32)],
        compiler_params=_cparams(("parallel", "arbitrary")),
        name="ffn_ln",
    )(xb, x, w1, w2, g.reshape(1, D), b.reshape(1, D))


def _sb_kernel(q_ref, k_ref, v_ref, o_ref, *, tq, tk, sub, scale):
    i = pl.program_id(2)
    q = q_ref[...]
    r = lax.broadcasted_iota(I32, (sub, sub), 0)
    c = lax.broadcasted_iota(I32, (sub, sub), 1)
    tri = jnp.where(r > c, 1.0, 0.0).astype(BF16)
    qpos = i * tq + lax.broadcasted_iota(I32, (tq, tk), 0)
    koff = lax.broadcasted_iota(I32, (tq, tk), 1)

    def chunk(kc, carry, acc, diag):
        start = pl.multiple_of(kc * tk, tk)
        k = k_ref[pl.ds(start, tk), :]
        v = v_ref[pl.ds(start, tk), :]
        z = _dot_nt(q, k) * scale
        softplus_neg_abs = jnp.log1p(jnp.exp(-jnp.abs(z)))
        log_beta = jnp.minimum(z, 0.0) - softplus_neg_abs
        log_keep = log_beta - z
        if diag:
            strict = (start + koff) < qpos
            log_keep = jnp.where(strict, log_keep, 0.0)
        laters = []
        for s in reversed(range(tk // sub)):
            lk = log_keep[:, s * sub:(s + 1) * sub]
            hi, lo = _split2(lk)
            later = _dot(hi, tri) + _dot(lo, tri) + carry
            laters.append(later)
            carry = later[:, 0:1] + lk[:, 0:1]
        later = jnp.concatenate(laters[::-1], axis=1)
        a = jnp.exp(log_beta + later)
        if diag:
            a = jnp.where(strict, a, 0.0)
        acc = acc + _dot(a.astype(BF16), v)
        return carry, acc

    kd = (i * tq + tq - 1) // tk
    carry = jnp.zeros((tq, 1), F32)
    acc = jnp.zeros((tq, q.shape[1]), F32)
    carry, acc = chunk(kd, carry, acc, True)
    carry, acc = lax.fori_loop(0, kd, lambda it, st: chunk(kd - 1 - it, st[0], st[1], False),
                               (carry, acc))
    o_ref[...] = acc.astype(o_ref.dtype)


def stick_breaking(h, B, S, q_col, k_col, v_col, n_heads, tq=256, tk=512, sub=256):
    dh = HEAD_DIM_SB
    tq, tk = min(tq, S), min(tk, S)
    sub = min(sub, tk)
    assert S % tq == 0 and S % tk == 0 and tk % sub == 0
    return pl.pallas_call(
        functools.partial(_sb_kernel, tq=tq, tk=tk, sub=sub, scale=dh ** -0.5),
        grid=(B, n_heads, S // tq),
        in_specs=[pl.BlockSpec((None, tq, dh), lambda b, hh, i: (b, i, q_col + hh)),
                  pl.BlockSpec((None, S, dh), lambda b, hh, i: (b, 0, k_col + hh)),
                  pl.BlockSpec((None, S, dh), lambda b, hh, i: (b, 0, v_col + hh))],
        out_specs=pl.BlockSpec((None, tq, dh), lambda b, hh, i: (b, i, hh)),
        out_shape=jax.ShapeDtypeStruct((B, S, n_heads * dh), BF16),
        compiler_params=_cparams(("parallel", "parallel", "arbitrary")),
        name="stick_breaking",
    )(h, h, h)


def _sel_kernel(qi_ref, ki_ref, wi_ref, o_ref, keys_ref, *, blk, nk, topk, idx_scale, w_scale):
    n = pl.program_id(1)
    nh = qi_ref.shape[0]
    qall = qi_ref[...].reshape(nh * blk, qi_ref.shape[2]) * idx_scale
    qh, ql = _split2(qall)
    w8 = wi_ref[...] * w_scale
    row = lax.broadcasted_iota(I32, (blk, blk), 0)
    col = lax.broadcasted_iota(I32, (blk, blk), 1)
    causal = row <= col
    tri = jnp.where(col < row, 1.0, 0.0).astype(BF16)

    def score_keys(kb, diag):
        start = pl.multiple_of(kb * blk, blk)
        kh, kl = _split2(ki_ref[pl.ds(start, blk), :])
        d = _dot_nt(kh, qh) + (_dot_nt(kh, ql) + _dot_nt(kl, qh))
        d = jnp.maximum(d, 0.0)
        sc = d[:, 0:blk] * w8[0:1, :]
        for hh in range(1, nh):
            sc = sc + d[:, hh * blk:(hh + 1) * blk] * w8[hh:hh + 1, :]
        sc = jnp.where(sc == 0.0, 0.0, sc)
        bits = pltpu.bitcast(sc, I32)
        key = jnp.where(bits < 0, bits ^ 0x7FFFFFFF, bits)
        if diag:
            key = jnp.where(causal, key, INT_MIN)
        keys_ref[kb] = key

    score_keys(n, True)
    lax.fori_loop(0, n, lambda kb, c: (score_keys(kb, False), c)[1], 0)

    def count(pred):
        def body(kb, c):
            return c + jnp.where(pred(keys_ref[kb]), 1, 0)
        c = lax.fori_loop(0, n + 1, body, jnp.zeros((blk, blk), I32))
        return jnp.sum(c, axis=0, keepdims=True)

    c0 = count(lambda key: key >= 0)
    theta = jnp.where(c0 >= topk, 0, INT_MIN).astype(I32)

    def bit_step(i, theta):
        cand = theta + lax.shift_left(jnp.int32(1), 30 - i)
        c = count(lambda key: key >= cand)
        return jnp.where(c >= topk, cand, theta)

    theta = lax.fori_loop(0, 31, bit_step, theta)
    need = (topk - count(lambda key: key > theta)).astype(F32)

    def emit(kb, carry, diag):
        key = keys_ref[kb]
        eq = key == theta
        eqf = jnp.where(eq, 1.0, 0.0)
        rank = _dot(tri, eqf.astype(BF16)) + carry
        sel = jnp.where(key > theta, 1.0, jnp.where(rank < need, eqf, 0.0))
        if diag:
            sel = jnp.where(causal, sel, 0.0)
        madd = jnp.where(sel > 0.5, 0.0, -jnp.inf)
        o_ref[kb] = madd.T.astype(o_ref.dtype)
        return carry + jnp.sum(eqf, axis=0, keepdims=True)

    carry = lax.fori_loop(0, n, lambda kb, c: emit(kb, c, False), jnp.zeros((1, blk), F32))
    emit(n, carry, True)

    def fill(kb, c):
        o_ref[kb] = jnp.full((blk, blk), -jnp.inf, o_ref.dtype)
        return c

    lax.fori_loop(n + 1, nk, fill, 0)


def dsa_select(qi, ki, wi, topk):
    B, nh, S, di = qi.shape
    blk = Q_BLOCK
    nq = S // blk
    return pl.pallas_call(
        functools.partial(_sel_kernel, blk=blk, nk=nq, topk=topk,
                          idx_scale=HEAD_DIM_IDX ** -0.5, w_scale=N_HEADS_IDX ** -0.5),
        grid=(B, nq),
        in_specs=[pl.BlockSpec((None, nh, blk, di), lambda b, i: (b, 0, i, 0)),
                  pl.BlockSpec((None, S, di), lambda b, i: (b, 0, 0)),
                  pl.BlockSpec((None, nh, blk), lambda b, i: (b, 0, i))],
        out_specs=pl.BlockSpec((None, None, nq, blk, blk), lambda b, i: (b, i, 0, 0, 0)),
        out_shape=jax.ShapeDtypeStruct((B, nq, nq, blk, blk), BF16),
        scratch_shapes=[pltpu.VMEM((nq, blk, blk), I32)],
        compiler_params=_cparams(("parallel", "arbitrary")),
        name="dsa_select",
    )(qi, ki, wi)


def _dsa_kernel(q_ref, k_ref, v_ref, m_ref, bias_ref, o_ref, *, tq, tk, dh):
    i = pl.program_id(2)
    blk = m_ref.shape[-1]
    rq, rk = tq // blk, tk // blk
    nb = bias_ref.shape[1] - 1
    q = q_ref[...] * (dh ** -0.5)
    lane = lax.broadcasted_iota(I32, q.shape, 1)
    halves = (lane < dh, lane >= dh)
    qs = [jnp.where(hm, q, jnp.zeros_like(q)) for hm in halves]

    def tiles(load):
        return jnp.concatenate(
            [jnp.concatenate([load(r, c) for c in range(rk)], axis=1) for r in range(rq)], axis=0)

    def body(kc, carry):
        start = pl.multiple_of(kc * tk, tk)
        k = k_ref[pl.ds(start, tk), :]
        v = v_ref[pl.ds(start, tk), :]
        madd = tiles(lambda r, c: m_ref[r, kc * rk + c]).astype(F32)
        out = []
        for hh in range(2):
            m, l, acc = carry[hh]
            bias = tiles(lambda r, c: bias_ref[hh, jnp.minimum(nb - (i * rq + r) + (kc * rk + c), nb)])
            s = _dot_nt(qs[hh], k) + bias + madd
            m_new = jnp.maximum(m, jnp.max(s, axis=1, keepdims=True))
            alpha = jnp.exp(m - m_new)
            p = jnp.exp(s - m_new)
            l = l * alpha + jnp.sum(p, axis=1, keepdims=True)
            acc = acc * alpha + _dot(p.astype(BF16), v)
            out.append((m_new, l, acc))
        return tuple(out)

    init = tuple((jnp.full((tq, 1), NEG, F32), jnp.zeros((tq, 1), F32),
                  jnp.zeros((tq, LANES), F32)) for _ in range(2))
    n_chunks = (i * tq + tq - 1) // tk + 1
    (_, l0, a0), (_, l1, a1) = lax.fori_loop(0, n_chunks, body, init)
    o_ref[...] = jnp.where(halves[0], a0 / l0, a1 / l1).astype(o_ref.dtype)


def dsa_attention(h, madd, bias_tiles, B, S, q_col, k_col, v_col, tq=256, tk=512):
    blk = Q_BLOCK
    nq = S // blk
    tq, tk = min(tq, S), min(tk, S)
    assert S % tq == 0 and S % tk == 0 and tq % blk == 0 and tk % blk == 0
    n_pairs = N_HEADS_DSA * HEAD_DIM_DSA // LANES
    return pl.pallas_call(
        functools.partial(_dsa_kernel, tq=tq, tk=tk, dh=HEAD_DIM_DSA),
        grid=(B, n_pairs, S // tq),
        in_specs=[pl.BlockSpec((None, tq, LANES), lambda b, p, i: (b, i, q_col + p)),
                  pl.BlockSpec((None, S, LANES), lambda b, p, i: (b, 0, k_col + p)),
                  pl.BlockSpec((None, S, LANES), lambda b, p, i: (b, 0, v_col + p)),
                  pl.BlockSpec((None, tq // blk, nq, blk, blk), lambda b, p, i: (b, i, 0, 0, 0)),
                  pl.BlockSpec((2, nq + 1, blk, blk), lambda b, p, i: (p, 0, 0, 0))],
        out_specs=pl.BlockSpec((None, tq, LANES), lambda b, p, i: (b, i, p)),
        out_shape=jax.ShapeDtypeStruct((B, S, n_pairs * LANES), BF16),
        compiler_params=_cparams(("parallel", "parallel", "arbitrary")),
        name="dsa_attention",
    )(h, h, h, madd, bias_tiles)


def _dil_kernel(q_ref, k_ref, v_ref, bias_ref, o_ref, qf, kf, vf, m_s, l_s, acc_s,
                *, blk, chunk, dils, scale):
    c = pl.program_id(2)
    base = pl.multiple_of(c * chunk, chunk)

    @pl.when(c == 0)
    def _():
        qf[...] = q_ref[...].astype(F32)
        kf[...] = k_ref[...].astype(F32)
        vf[...] = v_ref[...].astype(F32)

    row = lax.broadcasted_iota(I32, (blk, 2 * blk), 0)
    colw = lax.broadcasted_iota(I32, (blk, 2 * blk), 1)
    dist = blk + row - colw
    band = (dist >= 0) & (dist <= blk)

    for g, d in enumerate(dils):
        span = blk * d
        tiles = chunk // blk

        def body(t, carry, g=g, d=d, span=span):
            u = t // d
            r = t - u * d
            start = base + u * span + r
            has_prev = start >= span
            prev = jnp.where(has_prev, start - span, start)
            if d == 1:
                start = pl.multiple_of(start, blk)
                prev = pl.multiple_of(prev, blk)
                rows, prows = pl.ds(start, blk), pl.ds(prev, blk)
                lrows = pl.ds(pl.multiple_of(start - base, blk), blk)
            else:
                rows, prows = pl.ds(start, blk, stride=d), pl.ds(prev, blk, stride=d)
                lrows = pl.ds(start - base, blk, stride=d)
            q = qf[rows, :].astype(BF16)
            kk = jnp.concatenate([kf[prows, :], kf[rows, :]], axis=0).astype(BF16)
            vv = jnp.concatenate([vf[prows, :], vf[rows, :]], axis=0).astype(BF16)
            s = _dot_nt(q, kk) * scale + bias_ref[g]
            first_col = jnp.where(has_prev, 0, blk)
            s = jnp.where(band & (colw >= first_col), s, NEG)
            m_t = jnp.max(s, axis=1, keepdims=True)
            e = jnp.exp(s - m_t)
            l_t = jnp.sum(e, axis=1, keepdims=True)
            o_t = _dot(e.astype(BF16), vv)
            if g == 0:
                m_s[lrows, :] = jnp.broadcast_to(m_t, (blk, LANES))
                l_s[lrows, :] = jnp.broadcast_to(l_t, (blk, LANES))
                acc_s[lrows, :] = o_t
            else:
                m_old = m_s[lrows, :]
                m_new = jnp.maximum(m_old, m_t)
                a_old = jnp.exp(m_old - m_new)
                a_t = jnp.exp(m_t - m_new)
                m_s[lrows, :] = m_new
                l_s[lrows, :] = l_s[lrows, :] * a_old + l_t * a_t
                acc_s[lrows, :] = acc_s[lrows, :] * a_old + o_t * a_t
            return carry

        lax.fori_loop(0, tiles, body, 0)

    o_ref[...] = (acc_s[...] / l_s[...]).astype(o_ref.dtype)


def dilated_attention(h, bias_tiles, B, S, q_col, k_col, v_col, n_heads):
    blk = Q_BLOCK
    dils = tuple(d for _, d in DILATED_CONFIGS)
    assert all(w // d == blk for w, d in DILATED_CONFIGS)
    chunk = blk * max(dils)
    assert S % chunk == 0
    dh = HEAD_DIM_DIL
    return pl.pallas_call(
        functools.partial(_dil_kernel, blk=blk, chunk=chunk, dils=dils, scale=dh ** -0.5),
        grid=(B, n_heads, S // chunk),
        in_specs=[pl.BlockSpec((None, S, dh), lambda b, hh, c: (b, 0, q_col + hh)),
                  pl.BlockSpec((None, S, dh), lambda b, hh, c: (b, 0, k_col + hh)),
                  pl.BlockSpec((None, S, dh), lambda b, hh, c: (b, 0, v_col + hh)),
                  pl.BlockSpec((None, len(dils), blk, 2 * blk), lambda b, hh, c: (hh, 0, 0, 0))],
        out_specs=pl.BlockSpec((None, chunk, dh), lambda b, hh, c: (b, c, hh)),
        out_shape=jax.ShapeDtypeStruct((B, S, n_heads * dh), BF16),
        scratch_shapes=[pltpu.VMEM((S, dh), F32), pltpu.VMEM((S, dh), F32), pltpu.VMEM((S, dh), F32),
                        pltpu.VMEM((chunk, LANES), F32), pltpu.VMEM((chunk, LANES), F32),
                        pltpu.VMEM((chunk, dh), F32)],
        compiler_params=_cparams(("parallel", "parallel", "arbitrary")),
        name="dilated_attention",
    )(h, h, h, bias_tiles)


def _rel_bucket(dist):
    max_exact = N_BUCKETS // 2
    d_f = jnp.maximum(dist, 1).astype(F32)
    large = max_exact + (jnp.log(d_f / max_exact) / math.log(BUCKET_MAX_DIST / max_exact)
                         * (N_BUCKETS - max_exact)).astype(I32)
    large = jnp.minimum(large, N_BUCKETS - 1)
    return jnp.where(dist < max_exact, dist, large)


def _bias_by_distance(rel_bias, n):
    return rel_bias.astype(F32)[_rel_bucket(jnp.arange(n, dtype=I32))].T


def _toeplitz_band(g, nblk):
    blk = Q_BLOCK
    period = blk * (nblk + 2)
    m = jnp.arange(period, dtype=I32)
    m = jnp.where(m >= period - blk, m - period, m)
    v = g[:, jnp.clip(blk * nblk - m, 0, g.shape[1] - 1)]
    flat = jnp.tile(v, (1, blk + 1))[:, :blk * (period - 1)]
    return flat.reshape(g.shape[0], blk, period - 1)[:, :, :blk * (nblk + 1)]


def _causal_bias_tiles(rel_bias, nq):
    blk = Q_BLOCK
    band = _toeplitz_band(_bias_by_distance(rel_bias, nq * blk), nq)
    return band.reshape(band.shape[0], blk, nq + 1, blk).transpose(0, 2, 1, 3)


def _dilated_bias_tiles(rel_bias):
    blk = Q_BLOCK
    g = _bias_by_distance(rel_bias, 2 * blk * max(d for _, d in DILATED_CONFIGS))
    return jnp.stack([_toeplitz_band(g[:, ::d][:, :2 * blk], 1) for _, d in DILATED_CONFIGS], axis=1)


def even_mixer(x2, xb, w_in, w_out, rel_bias, B, S):
    half = N_HEADS_SB * HEAD_DIM_SB
    n_main = 6 * half
    n_qi = N_HEADS_IDX * HEAD_DIM_IDX
    slab = half // LANES
    h = matmul(xb, w_in[:, :n_main].astype(BF16), BF16).reshape(B, S, n_main)
    w_idx = w_in[:, n_main:]
    pad = -w_idx.shape[1] % LANES
    hi = matmul_split(x2, jnp.pad(w_idx, ((0, 0), (0, pad))))
    qi = hi[:, :n_qi].reshape(B, S, N_HEADS_IDX, HEAD_DIM_IDX).transpose(0, 2, 1, 3)
    ki = hi[:, n_qi:n_qi + HEAD_DIM_IDX].reshape(B, S, HEAD_DIM_IDX)
    wi = hi[:, n_qi + HEAD_DIM_IDX:n_qi + HEAD_DIM_IDX + N_HEADS_IDX].reshape(B, S, N_HEADS_IDX)
    wi = wi.transpose(0, 2, 1)
    madd = dsa_select(qi, ki, wi, min(TOPK_MAX, S // 4))
    oa = stick_breaking(h, B, S, 0, slab, 2 * slab, N_HEADS_SB)
    ob = dsa_attention(h, madd, _causal_bias_tiles(rel_bias, S // Q_BLOCK), B, S,
                       3 * slab, 4 * slab, 5 * slab)
    w_out = w_out.astype(BF16)
    return [oa.reshape(B * S, half), ob.reshape(B * S, half)], [w_out[:half], w_out[half:]]


def odd_mixer(xb, w_in, w_out, rel_bias, B, S):
    width = N_HEADS_DIL * HEAD_DIM_DIL
    h = matmul(xb, w_in.astype(BF16), BF16).reshape(B, S, 3 * width)
    o = dilated_attention(h, _dilated_bias_tiles(rel_bias), B, S, 0, N_HEADS_DIL, 2 * N_HEADS_DIL,
                          N_HEADS_DIL)
    return [o.reshape(B * S, width)], [w_out.astype(BF16)]


def kernel(x, even_w_in, even_w_out, odd_w_in, odd_w_out, rel_bias, ln_mix_g, ln_mix_b,
           ffn_w1, ffn_w2, ln_ffn_g, ln_ffn_b):
    B, S, D = x.shape
    x2 = x.reshape(B * S, D)
    xb = x2.astype(BF16)
    for layer in range(DEPTH):
        if layer % 2 == 0:
            a_list, w_list = even_mixer(x2, xb, even_w_in[layer // 2], even_w_out[layer // 2],
                                        rel_bias, B, S)
        else:
            a_list, w_list = odd_mixer(xb, odd_w_in[layer // 2], odd_w_out[layer // 2],
                                       rel_bias, B, S)
        x2, xb = proj_residual_ln(a_list, w_list, x2, ln_mix_g[layer], ln_mix_b[layer])
        x2, xb = ffn_residual_ln(x2, xb, ffn_w1[layer].astype(BF16), ffn_w2[layer].astype(BF16),
                                 ln_ffn_g[layer], ln_ffn_b[layer])
    return x2.reshape(B, S, D)
```

```python
import functools
import math

import jax
import jax.numpy as jnp
import numpy as np
from jax import lax
from jax.experimental import pallas as pl
from jax.experimental.pallas import tpu as pltpu

F32 = jnp.float32
BF16 = jnp.bfloat16
I32 = jnp.int32

Q_BLOCK = 128
HEAD_DIM_SB = 128
N_HEADS_SB = 8
N_HEADS_DSA = 16
HEAD_DIM_DSA = 64
N_HEADS_IDX = 8
HEAD_DIM_IDX = 64
TOPK_MAX = 256
N_HEADS_DIL = 16
HEAD_DIM_DIL = 128
DILATED_CONFIGS = ((128, 1), (512, 4), (2048, 16))
N_BUCKETS = 32
BUCKET_MAX_DIST = 2048
DEPTH = 2
DN_ALPHA = (2 * DEPTH) ** 0.25
LN_EPS = 1e-5
NEG = -1e30
INT_MIN = -(2 ** 31)
LOG2E = math.log2(math.e)

LANES = 128
VMEM_LIMIT_BYTES = 52 * 1024 * 1024


def _cparams(sem):
    return pltpu.CompilerParams(dimension_semantics=sem, vmem_limit_bytes=VMEM_LIMIT_BYTES)


def _dot(a, b):
    return jnp.dot(a, b, preferred_element_type=F32)


def _dot_nt(a, b):
    return lax.dot_general(a, b, (((1,), (1,)), ((), ())), preferred_element_type=F32)


def _split2(x):
    hi = x.astype(BF16)
    lo = (x - hi.astype(F32)).astype(BF16)
    return hi, lo


def _mm_kernel(a_ref, b_ref, s_ref, o_ref):
    acc = _dot(a_ref[...].astype(BF16), b_ref[...])
    o_ref[...] = (acc * s_ref[...]).astype(o_ref.dtype)


def matmul(a, b, col_scale, out_dtype, tm=1024, tn=1024):
    M, K = a.shape
    N = b.shape[1]
    tm, tn = min(tm, M), min(tn, N)
    assert M % tm == 0 and N % tn == 0
    return pl.pallas_call(
        _mm_kernel,
        grid=(M // tm, N // tn),
        in_specs=[pl.BlockSpec((tm, K), lambda i, j: (i, 0)),
                  pl.BlockSpec((K, tn), lambda i, j: (0, j)),
                  pl.BlockSpec((1, tn), lambda i, j: (0, j))],
        out_specs=pl.BlockSpec((tm, tn), lambda i, j: (i, j)),
        out_shape=jax.ShapeDtypeStruct((M, N), out_dtype),
        compiler_params=_cparams(("parallel", "arbitrary")),
        name="proj_matmul",
    )(a, b, col_scale.reshape(1, N))


def _col_scale(n, scaled):
    s = np.ones((n,), np.float32)
    for lo, hi, f in scaled:
        s[lo:hi] = f
    return jnp.asarray(s)


def _mm3_kernel(a_ref, bh_ref, bl_ref, o_ref):
    ah, al = _split2(a_ref[...])
    bh = bh_ref[...]
    o_ref[...] = _dot(ah, bh) + (_dot(ah, bl_ref[...]) + _dot(al, bh))


def matmul_split(a, b, tm=512):
    M, K = a.shape
    N = b.shape[1]
    tm = min(tm, M)
    assert M % tm == 0
    bh, bl = _split2(b)
    return pl.pallas_call(
        _mm3_kernel,
        grid=(M // tm,),
        in_specs=[pl.BlockSpec((tm, K), lambda i: (i, 0)),
                  pl.BlockSpec((K, N), lambda i: (0, 0)),
                  pl.BlockSpec((K, N), lambda i: (0, 0))],
        out_specs=pl.BlockSpec((tm, N), lambda i: (i, 0)),
        out_shape=jax.ShapeDtypeStruct((M, N), F32),
        compiler_params=_cparams(("parallel",)),
        name="indexer_proj",
    )(a, bh, bl)


def _layer_norm(y, g, b):
    mu = jnp.mean(y, axis=-1, keepdims=True)
    yc = y - mu
    var = jnp.mean(jnp.square(yc), axis=-1, keepdims=True)
    return yc * lax.rsqrt(var + LN_EPS) * g + b


def _proj_ln_kernel(*refs, n_in):
    a_refs, w_refs = refs[:n_in], refs[n_in:2 * n_in]
    x_ref, g_ref, b_ref, y_ref, yb_ref = refs[2 * n_in:]
    acc = _dot(a_refs[0][...], w_refs[0][...])
    for a_ref, w_ref in zip(a_refs[1:], w_refs[1:]):
        acc = acc + _dot(a_ref[...], w_ref[...])
    y = _layer_norm(DN_ALPHA * x_ref[...] + acc, g_ref[...], b_ref[...])
    y_ref[...] = y
    yb_ref[...] = y.astype(BF16)


def proj_residual_ln(a_list, w_list, x, g, b, tm=512):
    M, D = x.shape
    tm = min(tm, M)
    assert M % tm == 0
    n_in = len(a_list)
    in_specs = ([pl.BlockSpec((tm, a.shape[1]), lambda i: (i, 0)) for a in a_list]
                + [pl.BlockSpec(w.shape, lambda i: (0, 0)) for w in w_list]
                + [pl.BlockSpec((tm, D), lambda i: (i, 0)),
                   pl.BlockSpec((1, D), lambda i: (0, 0)),
                   pl.BlockSpec((1, D), lambda i: (0, 0))])
    return pl.pallas_call(
        functools.partial(_proj_ln_kernel, n_in=n_in),
        grid=(M // tm,),
        in_specs=in_specs,
        out_specs=[pl.BlockSpec((tm, D), lambda i: (i, 0)),
                   pl.BlockSpec((tm, D), lambda i: (i, 0))],
        out_shape=[jax.ShapeDtypeStruct((M, D), F32), jax.ShapeDtypeStruct((M, D), BF16)],
        compiler_params=_cparams(("parallel",)),
        name="out_proj_ln",
    )(*a_list, *w_list, x, g.reshape(1, D), b.reshape(1, D))


def _ffn_kernel(xb_ref, x_ref, w1_ref, w2_ref, g_ref, b_ref, y_ref, yb_ref, acc_ref):
    f = pl.program_id(1)

    @pl.when(f == 0)
    def _():
        acc_ref[...] = jnp.zeros_like(acc_ref)

    h = _dot(xb_ref[...], w1_ref[...])
    h = jnp.square(jnp.maximum(h, 0.0)).astype(BF16)
    acc_ref[...] += _dot(h, w2_ref[...])

    @pl.when(f == pl.num_programs(1) - 1)
    def _():
        y = _layer_norm(DN_ALPHA * x_ref[...] + acc_ref[...], g_ref[...], b_ref[...])
        y_ref[...] = y
        yb_ref[...] = y.astype(BF16)


def ffn_residual_ln(x, xb, w1, w2, g, b, tm=512, tf=1024):
    M, D = x.shape
    F = w1.shape[1]
    tm, tf = min(tm, M), min(tf, F)
    assert M % tm == 0 and F % tf == 0
    return pl.pallas_call(
        _ffn_kernel,
        grid=(M // tm, F // tf),
        in_specs=[pl.BlockSpec((tm, D), lambda i, f: (i, 0)),
                  pl.BlockSpec((tm, D), lambda i, f: (i, 0)),
                  pl.BlockSpec((D, tf), lambda i, f: (0, f)),
                  pl.BlockSpec((tf, D), lambda i, f: (f, 0)),
                  pl.BlockSpec((1, D), lambda i, f: (0, 0)),
                  pl.BlockSpec((1, D), lambda i, f: (0, 0))],
        out_specs=[pl.BlockSpec((tm, D), lambda i, f: (i, 0)),
                   pl.BlockSpec((tm, D), lambda i, f: (i, 0))],
        out_shape=[jax.ShapeDtypeStruct((M, D), F32), jax.ShapeDtypeStruct((M, D), BF16)],
        scratch_shapes=[pltpu.VMEM((tm, D), F32)],
        compiler_params=_cparams(("parallel", "arbitrary")),
        name="ffn_ln",
    )(xb, x, w1, w2, g.reshape(1, D), b.reshape(1, D))


def _sb_kernel(q_ref, k_ref, v_ref, o_ref, *, tq, tk, sub):
    i = pl.program_id(2)
    q = q_ref[...]
    r = lax.broadcasted_iota(I32, (sub, sub), 0)
    c = lax.broadcasted_iota(I32, (sub, sub), 1)
    tri = jnp.where(r > c, 1.0, 0.0).astype(BF16)
    qpos = i * tq + lax.broadcasted_iota(I32, (tq, tk), 0)
    koff = lax.broadcasted_iota(I32, (tq, tk), 1)

    def chunk(kc, carry, acc, diag):
        start = pl.multiple_of(kc * tk, tk)
        k = k_ref[pl.ds(start, tk), :]
        v = v_ref[pl.ds(start, tk), :]
        z = _dot_nt(q, k)
        softplus = jnp.log(1.0 + jnp.exp2(-jnp.abs(z))) * LOG2E
        log_beta = jnp.minimum(z, 0.0) - softplus
        log_keep = log_beta - z
        if diag:
            strict = (start + koff) < qpos
            log_keep = jnp.where(strict, log_keep, 0.0)
        laters = []
        for s in reversed(range(tk // sub)):
            lk = log_keep[:, s * sub:(s + 1) * sub]
            later = _dot(lk.astype(BF16), tri) + carry
            laters.append(later)
            carry = later[:, 0:1] + lk[:, 0:1]
        later = jnp.concatenate(laters[::-1], axis=1)
        a = jnp.exp2(log_beta + later)
        if diag:
            a = jnp.where(strict, a, 0.0)
        acc = acc + _dot(a.astype(BF16), v)
        return carry, acc

    kd = (i * tq + tq - 1) // tk
    carry = jnp.zeros((tq, 1), F32)
    acc = jnp.zeros((tq, q.shape[1]), F32)
    carry, acc = chunk(kd, carry, acc, True)
    carry, acc = lax.fori_loop(0, kd, lambda it, st: chunk(kd - 1 - it, st[0], st[1], False),
                               (carry, acc))
    o_ref[...] = acc.astype(o_ref.dtype)


def stick_breaking(h, B, S, q_col, k_col, v_col, n_heads, tq=512, tk=512, sub=256):
    dh = HEAD_DIM_SB
    tq, tk = min(tq, S), min(tk, S)
    sub = min(sub, tk)
    assert S % tq == 0 and S % tk == 0 and tk % sub == 0
    return pl.pallas_call(
        functools.partial(_sb_kernel, tq=tq, tk=tk, sub=sub),
        grid=(B, n_heads, S // tq),
        in_specs=[pl.BlockSpec((None, tq, dh), lambda b, hh, i: (b, i, q_col + hh)),
                  pl.BlockSpec((None, S, dh), lambda b, hh, i: (b, 0, k_col + hh)),
                  pl.BlockSpec((None, S, dh), lambda b, hh, i: (b, 0, v_col + hh))],
        out_specs=pl.BlockSpec((None, tq, dh), lambda b, hh, i: (b, i, hh)),
        out_shape=jax.ShapeDtypeStruct((B, S, n_heads * dh), BF16),
        compiler_params=_cparams(("parallel", "parallel", "arbitrary")),
        name="stick_breaking",
    )(h, h, h)


def _sel_kernel(qi_ref, ki_ref, wi_ref, o_ref, keys_ref, *, blk, nk, ch, topk, idx_scale, w_scale):
    n = pl.program_id(1)
    nh = qi_ref.shape[0]
    cw = ch * blk
    last = n // ch
    qall = qi_ref[...].reshape(nh * blk, qi_ref.shape[2]) * idx_scale
    qh, ql = _split2(qall)
    w8 = wi_ref[...] * w_scale
    row = lax.broadcasted_iota(I32, (blk, blk), 0)
    col = lax.broadcasted_iota(I32, (blk, blk), 1)
    tri = jnp.where(col < row, 1.0, 0.0).astype(BF16)
    qpos = n * blk + lax.broadcasted_iota(I32, (cw, blk), 1)
    koff = lax.broadcasted_iota(I32, (cw, blk), 0)

    def score_keys(cc, diag):
        start = pl.multiple_of(cc * cw, cw)
        kh, kl = _split2(ki_ref[pl.ds(start, cw), :])
        d = _dot_nt(kh, qh) + (_dot_nt(kh, ql) + _dot_nt(kl, qh))
        d = jnp.maximum(d, 0.0)
        sc = d[:, 0:blk] * w8[0:1, :]
        for hh in range(1, nh):
            sc = sc + d[:, hh * blk:(hh + 1) * blk] * w8[hh:hh + 1, :]
        sc = jnp.where(sc == 0.0, 0.0, sc)
        bits = pltpu.bitcast(sc, I32)
        key = jnp.where(bits < 0, bits ^ 0x7FFFFFFF, bits)
        if diag:
            key = jnp.where(start + koff <= qpos, key, INT_MIN)
        keys_ref[pl.ds(cc * ch, ch)] = key.reshape(ch, blk, blk)

    score_keys(last, True)
    lax.fori_loop(0, last, lambda cc, c: (score_keys(cc, False), c)[1], 0)

    def count(pred):
        def body(cc, c):
            kk = keys_ref[pl.ds(cc * ch, ch)]
            for j in range(ch):
                c = c + jnp.where(pred(kk[j]), 1, 0)
            return c
        c = lax.fori_loop(0, last + 1, body, jnp.zeros((blk, blk), I32))
        return jnp.sum(c, axis=0, keepdims=True)

    c0 = count(lambda key: key >= 0)
    theta = jnp.where(c0 >= topk, 0, INT_MIN).astype(I32)

    def bit_step(i, theta):
        cand = theta + lax.shift_left(jnp.int32(1), 30 - i)
        c = count(lambda key: key >= cand)
        return jnp.where(c >= topk, cand, theta)

    theta = lax.fori_loop(0, 31, bit_step, theta)
    need = (topk - count(lambda key: key > theta)).astype(F32)

    def emit(cc, carry, diag):
        kk = keys_ref[pl.ds(cc * ch, ch)]
        for j in range(ch):
            key = kk[j]
            eqf = jnp.where(key == theta, 1.0, 0.0)
            rank = _dot(tri, eqf.astype(BF16)) + carry
            sel = jnp.where(key > theta, 1.0, jnp.where(rank < need, eqf, 0.0))
            if diag:
                kb = cc * ch + j
                sel = jnp.where(kb * blk + row <= n * blk + col, sel, 0.0)
            madd = jnp.where(sel > 0.5, 0.0, -jnp.inf)
            o_ref[cc * ch + j] = madd.T.astype(o_ref.dtype)
            carry = carry + jnp.sum(eqf, axis=0, keepdims=True)
        return carry

    carry = lax.fori_loop(0, last, lambda cc, c: emit(cc, c, False), jnp.zeros((1, blk), F32))
    emit(last, carry, True)

    def fill(kb, c):
        o_ref[kb] = jnp.full((blk, blk), -jnp.inf, o_ref.dtype)
        return c

    lax.fori_loop((last + 1) * ch, nk, fill, 0)


def dsa_select(qi, ki, wi, topk, ch=4):
    B, nh, S, di = qi.shape
    blk = Q_BLOCK
    nq = S // blk
    ch = min(ch, nq)
    assert nq % ch == 0
    return pl.pallas_call(
        functools.partial(_sel_kernel, blk=blk, nk=nq, ch=ch, topk=topk,
                          idx_scale=HEAD_DIM_IDX ** -0.5, w_scale=N_HEADS_IDX ** -0.5),
        grid=(B, nq),
        in_specs=[pl.BlockSpec((None, nh, blk, di), lambda b, i: (b, 0, i, 0)),
                  pl.BlockSpec((None, S, di), lambda b, i: (b, 0, 0)),
                  pl.BlockSpec((None, nh, blk), lambda b, i: (b, 0, i))],
        out_specs=pl.BlockSpec((None, None, nq, blk, blk), lambda b, i: (b, i, 0, 0, 0)),
        out_shape=jax.ShapeDtypeStruct((B, nq, nq, blk, blk), BF16),
        scratch_shapes=[pltpu.VMEM((nq, blk, blk), I32)],
        compiler_params=_cparams(("parallel", "arbitrary")),
        name="dsa_select",
    )(qi, ki, wi)


def _dsa_kernel(q_ref, k_ref, v_ref, m_ref, bias_ref, o_ref, *, tq, tk, dh):
    i = pl.program_id(2)
    blk = m_ref.shape[-1]
    rq, rk = tq // blk, tk // blk
    nb = bias_ref.shape[1] - 1
    q = q_ref[...]
    lane = lax.broadcasted_iota(I32, q.shape, 1)
    halves = (lane < dh, lane >= dh)
    qs = [jnp.where(hm, q, jnp.zeros_like(q)) for hm in halves]
    vlane = lax.broadcasted_iota(I32, (tk, LANES), 1)
    vhalves = (vlane < dh, vlane >= dh)

    def tiles(load):
        return jnp.concatenate(
            [jnp.concatenate([load(r, c) for c in range(rk)], axis=1) for r in range(rq)], axis=0)

    def body(kc, carry):
        start = pl.multiple_of(kc * tk, tk)
        k = k_ref[pl.ds(start, tk), :]
        v = v_ref[pl.ds(start, tk), :]
        madd = tiles(lambda r, c: m_ref[r, kc * rk + c]).astype(F32)
        out = []
        for hh in range(2):
            m, acc = carry[hh]
            bias = tiles(lambda r, c: bias_ref[hh, jnp.minimum(nb - (i * rq + r) + (kc * rk + c), nb)])
            s = _dot_nt(qs[hh], k) + bias + madd
            m_new = jnp.maximum(m, jnp.max(s, axis=1, keepdims=True))
            p = jnp.exp2(s - m_new)
            vh = jnp.where(vhalves[hh], v, jnp.ones_like(v))
            acc = acc * jnp.exp2(m - m_new) + _dot(p.astype(BF16), vh)
            out.append((m_new, acc))
        return tuple(out)

    init = tuple((jnp.full((tq, 1), NEG, F32), jnp.zeros((tq, LANES), F32)) for _ in range(2))
    n_chunks = (i * tq + tq - 1) // tk + 1
    (_, a0), (_, a1) = lax.fori_loop(0, n_chunks, body, init)
    o_ref[...] = jnp.where(halves[0], a0 / a0[:, dh:dh + 1], a1 / a1[:, 0:1]).astype(o_ref.dtype)


def dsa_attention(h, madd, bias_tiles, B, S, q_col, k_col, v_col, tq=512, tk=1024):
    blk = Q_BLOCK
    nq = S // blk
    tq, tk = min(tq, S), min(tk, S)
    assert S % tq == 0 and S % tk == 0 and tq % blk == 0 and tk % blk == 0
    n_pairs = N_HEADS_DSA * HEAD_DIM_DSA // LANES
    return pl.pallas_call(
        functools.partial(_dsa_kernel, tq=tq, tk=tk, dh=HEAD_DIM_DSA),
        grid=(B, n_pairs, S // tq),
        in_specs=[pl.BlockSpec((None, tq, LANES), lambda b, p, i: (b, i, q_col + p)),
                  pl.BlockSpec((None, S, LANES), lambda b, p, i: (b, 0, k_col + p)),
                  pl.BlockSpec((None, S, LANES), lambda b, p, i: (b, 0, v_col + p)),
                  pl.BlockSpec((None, tq // blk, nq, blk, blk), lambda b, p, i: (b, i, 0, 0, 0)),
                  pl.BlockSpec((2, nq + 1, blk, blk), lambda b, p, i: (p, 0, 0, 0))],
        out_specs=pl.BlockSpec((None, tq, LANES), lambda b, p, i: (b, i, p)),
        out_shape=jax.ShapeDtypeStruct((B, S, n_pairs * LANES), BF16),
        compiler_params=_cparams(("parallel", "parallel", "arbitrary")),
        name="dsa_attention",
    )(h, h, h, madd, bias_tiles)


def _dil_kernel(q_ref, k_ref, v_ref, bias_ref, o_ref, qf, kf, vf, m_s, l_s, acc_s,
                *, blk, chunk, dils, unroll):
    c = pl.program_id(2)
    base = pl.multiple_of(c * chunk, chunk)

    @pl.when(c == 0)
    def _():
        qf[...] = q_ref[...].astype(F32)
        kf[...] = k_ref[...].astype(F32)
        vf[...] = v_ref[...].astype(F32)

    for g, d in enumerate(dils):
        span = blk * d

        def group(it, carry, g=g, d=d, span=span):
            for jj in range(unroll):
                t = it * unroll + jj
                u = t // d
                r = t - u * d
                start = base + u * span + r
                has_prev = start >= span
                prev = jnp.where(has_prev, start - span, start)
                if d == 1:
                    start = pl.multiple_of(start, blk)
                    prev = pl.multiple_of(prev, blk)
                    rows, prows = pl.ds(start, blk), pl.ds(prev, blk)
                    lrows = pl.ds(pl.multiple_of(start - base, blk), blk)
                else:
                    rows, prows = pl.ds(start, blk, stride=d), pl.ds(prev, blk, stride=d)
                    lrows = pl.ds(start - base, blk, stride=d)
                q = qf[rows, :].astype(BF16)
                kk = jnp.concatenate([kf[prows, :], kf[rows, :]], axis=0).astype(BF16)
                vv = jnp.concatenate([vf[prows, :], vf[rows, :]], axis=0).astype(BF16)
                s = _dot_nt(q, kk) + bias_ref[g, jnp.where(has_prev, 0, 1)]
                m_t = jnp.max(s, axis=1, keepdims=True)
                e = jnp.exp2(s - m_t)
                l_t = jnp.sum(e, axis=1, keepdims=True)
                m_s[g, lrows, :] = jnp.broadcast_to(m_t, (blk, LANES))
                l_s[g, lrows, :] = jnp.broadcast_to(l_t, (blk, LANES))
                acc_s[g, lrows, :] = _dot(e.astype(BF16), vv)
            return carry

        lax.fori_loop(0, chunk // (blk * unroll), group, 0)

    m = m_s[0]
    for g in range(1, len(dils)):
        m = jnp.maximum(m, m_s[g])
    num = den = None
    for g in range(len(dils)):
        w = jnp.exp2(m_s[g] - m)
        num = acc_s[g] * w if num is None else num + acc_s[g] * w
        den = l_s[g] * w if den is None else den + l_s[g] * w
    o_ref[...] = (num / den).astype(o_ref.dtype)


def dilated_attention(h, bias_tiles, B, S, q_col, k_col, v_col, n_heads, unroll=16):
    blk = Q_BLOCK
    dils = tuple(d for _, d in DILATED_CONFIGS)
    assert all(w // d == blk for w, d in DILATED_CONFIGS)
    chunk = blk * max(dils)
    assert S % chunk == 0 and (chunk // blk) % unroll == 0
    dh = HEAD_DIM_DIL
    ng = len(dils)
    return pl.pallas_call(
        functools.partial(_dil_kernel, blk=blk, chunk=chunk, dils=dils, unroll=unroll),
        grid=(B, n_heads, S // chunk),
        in_specs=[pl.BlockSpec((None, S, dh), lambda b, hh, c: (b, 0, q_col + hh)),
                  pl.BlockSpec((None, S, dh), lambda b, hh, c: (b, 0, k_col + hh)),
                  pl.BlockSpec((None, S, dh), lambda b, hh, c: (b, 0, v_col + hh)),
                  pl.BlockSpec((None, ng, 2, blk, 2 * blk), lambda b, hh, c: (hh, 0, 0, 0, 0))],
        out_specs=pl.BlockSpec((None, chunk, dh), lambda b, hh, c: (b, c, hh)),
        out_shape=jax.ShapeDtypeStruct((B, S, n_heads * dh), BF16),
        scratch_shapes=[pltpu.VMEM((S, dh), F32), pltpu.VMEM((S, dh), F32), pltpu.VMEM((S, dh), F32),
                        pltpu.VMEM((ng, chunk, LANES), F32), pltpu.VMEM((ng, chunk, LANES), F32),
                        pltpu.VMEM((ng, chunk, dh), F32)],
        compiler_params=_cparams(("parallel", "parallel", "arbitrary")),
        name="dilated_attention",
    )(h, h, h, bias_tiles)


def _rel_bucket(dist):
    max_exact = N_BUCKETS // 2
    d_f = jnp.maximum(dist, 1).astype(F32)
    large = max_exact + (jnp.log(d_f / max_exact) / math.log(BUCKET_MAX_DIST / max_exact)
                         * (N_BUCKETS - max_exact)).astype(I32)
    large = jnp.minimum(large, N_BUCKETS - 1)
    return jnp.where(dist < max_exact, dist, large)


def _bias_by_distance(rel_bias, n):
    return rel_bias.astype(F32)[_rel_bucket(jnp.arange(n, dtype=I32))].T


def _toeplitz_band(g, nblk):
    blk = Q_BLOCK
    period = blk * (nblk + 2)
    m = jnp.arange(period, dtype=I32)
    m = jnp.where(m >= period - blk, m - period, m)
    v = g[:, jnp.clip(blk * nblk - m, 0, g.shape[1] - 1)]
    flat = jnp.tile(v, (1, blk + 1))[:, :blk * (period - 1)]
    return flat.reshape(g.shape[0], blk, period - 1)[:, :, :blk * (nblk + 1)]


def _causal_bias_tiles(rel_bias, nq):
    blk = Q_BLOCK
    band = _toeplitz_band(_bias_by_distance(rel_bias, nq * blk) * LOG2E, nq)
    return band.reshape(band.shape[0], blk, nq + 1, blk).transpose(0, 2, 1, 3)


def _dilated_bias_tiles(rel_bias):
    blk = Q_BLOCK
    g = _bias_by_distance(rel_bias, 2 * blk * max(d for _, d in DILATED_CONFIGS)) * LOG2E
    j = blk + np.arange(blk)[:, None] - np.arange(2 * blk)[None, :]
    band = (j >= 0) & (j <= blk)
    masks = np.stack([band, band & (np.arange(2 * blk)[None, :] >= blk)])
    tiles = jnp.stack([_toeplitz_band(g[:, ::d][:, :2 * blk], 1) for _, d in DILATED_CONFIGS], axis=1)
    return jnp.where(masks[None, None], tiles[:, :, None], NEG)


def even_mixer(x2, xb, w_in, w_out, rel_bias, B, S):
    half = N_HEADS_SB * HEAD_DIM_SB
    n_main = 6 * half
    n_qi = N_HEADS_IDX * HEAD_DIM_IDX
    slab = half // LANES
    col_scale = _col_scale(n_main, [(0, half, HEAD_DIM_SB ** -0.5 * LOG2E),
                                    (3 * half, 4 * half, HEAD_DIM_DSA ** -0.5 * LOG2E)])
    h = matmul(xb, w_in[:, :n_main].astype(BF16), col_scale, BF16).reshape(B, S, n_main)
    w_idx = w_in[:, n_main:]
    pad = -w_idx.shape[1] % LANES
    hi = matmul_split(x2, jnp.pad(w_idx, ((0, 0), (0, pad))))
    qi = hi[:, :n_qi].reshape(B, S, N_HEADS_IDX, HEAD_DIM_IDX).transpose(0, 2, 1, 3)
    ki = hi[:, n_qi:n_qi + HEAD_DIM_IDX].reshape(B, S, HEAD_DIM_IDX)
    wi = hi[:, n_qi + HEAD_DIM_IDX:n_qi + HEAD_DIM_IDX + N_HEADS_IDX].reshape(B, S, N_HEADS_IDX)
    wi = wi.transpose(0, 2, 1)
    madd = dsa_select(qi, ki, wi, min(TOPK_MAX, S // 4))
    oa = stick_breaking(h, B, S, 0, slab, 2 * slab, N_HEADS_SB)
    ob = dsa_attention(h, madd, _causal_bias_tiles(rel_bias, S // Q_BLOCK), B, S,
                       3 * slab, 4 * slab, 5 * slab)
    w_out = w_out.astype(BF16)
    return [oa.reshape(B * S, half), ob.reshape(B * S, half)], [w_out[:half], w_out[half:]]


def odd_mixer(xb, w_in, w_out, rel_bias, B, S):
    width = N_HEADS_DIL * HEAD_DIM_DIL
    col_scale = _col_scale(3 * width, [(0, width, HEAD_DIM_DIL ** -0.5 * LOG2E)])
    h = matmul(xb, w_in.astype(BF16), col_scale, BF16).reshape(B, S, 3 * width)
    o = dilated_attention(h, _dilated_bias_tiles(rel_bias), B, S, 0, N_HEADS_DIL, 2 * N_HEADS_DIL,
                          N_HEADS_DIL)
    return [o.reshape(B * S, width)], [w_out.astype(BF16)]


def kernel(x, even_w_in, even_w_out, odd_w_in, odd_w_out, rel_bias, ln_mix_g, ln_mix_b,
           ffn_w1, ffn_w2, ln_ffn_g, ln_ffn_b):
    B, S, D = x.shape
    x2 = x.reshape(B * S, D)
    xb = x2.astype(BF16)
    for layer in range(DEPTH):
        if layer % 2 == 0:
            a_list, w_list = even_mixer(x2, xb, even_w_in[layer // 2], even_w_out[layer // 2],
                                        rel_bias, B, S)
        else:
            a_list, w_list = odd_mixer(xb, odd_w_in[layer // 2], odd_w_out[layer // 2],
                                       rel_bias, B, S)
        x2, xb = proj_residual_ln(a_list, w_list, x2, ln_mix_g[layer], ln_mix_b[layer])
        x2, xb = ffn_residual_ln(x2, xb, ffn_w1[layer].astype(BF16), ffn_w2[layer].astype(BF16),
                                 ln_ffn_g[layer], ln_ffn_b[layer])
    return x2.reshape(B, S, D)
```

```python
import functools
import math

import jax
import jax.numpy as jnp
import numpy as np
from jax import lax
from jax.experimental import pallas as pl
from jax.experimental.pallas import tpu as pltpu

F32 = jnp.float32
BF16 = jnp.bfloat16
I32 = jnp.int32

Q_BLOCK = 128
HEAD_DIM_SB = 128
N_HEADS_SB = 8
N_HEADS_DSA = 16
HEAD_DIM_DSA = 64
N_HEADS_IDX = 8
HEAD_DIM_IDX = 64
TOPK_MAX = 256
N_HEADS_DIL = 16
HEAD_DIM_DIL = 128
DILATED_CONFIGS = ((128, 1), (512, 4), (2048, 16))
N_BUCKETS = 32
BUCKET_MAX_DIST = 2048
DEPTH = 2
DN_ALPHA = (2 * DEPTH) ** 0.25
LN_EPS = 1e-5
NEG = -1e30
INT_MIN = -(2 ** 31)
LOG2E = math.log2(math.e)

LANES = 128
VMEM_LIMIT_BYTES = 52 * 1024 * 1024


def _cparams(sem):
    return pltpu.CompilerParams(dimension_semantics=sem, vmem_limit_bytes=VMEM_LIMIT_BYTES)


def _dot(a, b):
    return jnp.dot(a, b, preferred_element_type=F32)


def _dot_nt(a, b):
    return lax.dot_general(a, b, (((1,), (1,)), ((), ())), preferred_element_type=F32)


def _mm_kernel(a_ref, b_ref, s_ref, o_ref):
    acc = _dot(a_ref[...].astype(BF16), b_ref[...])
    o_ref[...] = (acc * s_ref[...]).astype(o_ref.dtype)


def matmul(a, b, col_scale, out_dtype, tm=1024, tn=1024):
    M, K = a.shape
    N = b.shape[1]
    tm, tn = min(tm, M), min(tn, N)
    assert M % tm == 0 and N % tn == 0
    return pl.pallas_call(
        _mm_kernel,
        grid=(M // tm, N // tn),
        in_specs=[pl.BlockSpec((tm, K), lambda i, j: (i, 0)),
                  pl.BlockSpec((K, tn), lambda i, j: (0, j)),
                  pl.BlockSpec((1, tn), lambda i, j: (0, j))],
        out_specs=pl.BlockSpec((tm, tn), lambda i, j: (i, j)),
        out_shape=jax.ShapeDtypeStruct((M, N), out_dtype),
        compiler_params=_cparams(("parallel", "arbitrary")),
        name="proj_matmul",
    )(a, b, col_scale.reshape(1, N))


def _col_scale(n, scaled):
    s = np.ones((n,), np.float32)
    for lo, hi, f in scaled:
        s[lo:hi] = f
    return jnp.asarray(s)


def _mm_slab_kernel(a_ref, b_ref, s_ref, o_ref):
    acc = _dot(a_ref[...].astype(BF16), b_ref[...]) * s_ref[...]
    for j in range(o_ref.shape[0]):
        o_ref[j] = acc[:, j * LANES:(j + 1) * LANES]


def matmul_slabs(a, b, col_scale, B, S, tm=1024, tn=256):
    M, K = a.shape
    N = b.shape[1]
    tm, tn = min(tm, S), min(tn, N)
    assert M == B * S and S % tm == 0 and N % tn == 0 and tn % LANES == 0
    steps = S // tm
    return pl.pallas_call(
        _mm_slab_kernel,
        grid=(B, steps, N // tn),
        in_specs=[pl.BlockSpec((tm, K), lambda b, i, j: (b * steps + i, 0)),
                  pl.BlockSpec((K, tn), lambda b, i, j: (0, j)),
                  pl.BlockSpec((1, tn), lambda b, i, j: (0, j))],
        out_specs=pl.BlockSpec((None, tn // LANES, tm, LANES), lambda b, i, j: (b, j, i, 0)),
        out_shape=jax.ShapeDtypeStruct((B, N // LANES, S, LANES), F32),
        compiler_params=_cparams(("parallel", "parallel", "arbitrary")),
        name="indexer_proj",
    )(a, b, col_scale.reshape(1, N))


def _layer_norm(y, g, b):
    mu = jnp.mean(y, axis=-1, keepdims=True)
    yc = y - mu
    var = jnp.mean(jnp.square(yc), axis=-1, keepdims=True)
    return yc * lax.rsqrt(var + LN_EPS) * g + b


def _proj_ln_kernel(*refs, n_in):
    a_refs, w_refs = refs[:n_in], refs[n_in:2 * n_in]
    x_ref, g_ref, b_ref, y_ref, yb_ref = refs[2 * n_in:]
    acc = _dot(a_refs[0][...], w_refs[0][...])
    for a_ref, w_ref in zip(a_refs[1:], w_refs[1:]):
        acc = acc + _dot(a_ref[...], w_ref[...])
    y = _layer_norm(DN_ALPHA * x_ref[...] + acc, g_ref[...], b_ref[...])
    y_ref[...] = y
    yb_ref[...] = y.astype(BF16)


def proj_residual_ln(a_list, w_list, x, g, b, tm=512):
    M, D = x.shape
    tm = min(tm, M)
    assert M % tm == 0
    n_in = len(a_list)
    in_specs = ([pl.BlockSpec((tm, a.shape[1]), lambda i: (i, 0)) for a in a_list]
                + [pl.BlockSpec(w.shape, lambda i: (0, 0)) for w in w_list]
                + [pl.BlockSpec((tm, D), lambda i: (i, 0)),
                   pl.BlockSpec((1, D), lambda i: (0, 0)),
                   pl.BlockSpec((1, D), lambda i: (0, 0))])
    return pl.pallas_call(
        functools.partial(_proj_ln_kernel, n_in=n_in),
        grid=(M // tm,),
        in_specs=in_specs,
        out_specs=[pl.BlockSpec((tm, D), lambda i: (i, 0)),
                   pl.BlockSpec((tm, D), lambda i: (i, 0))],
        out_shape=[jax.ShapeDtypeStruct((M, D), F32), jax.ShapeDtypeStruct((M, D), BF16)],
        compiler_params=_cparams(("parallel",)),
        name="out_proj_ln",
    )(*a_list, *w_list, x, g.reshape(1, D), b.reshape(1, D))


def _ffn_kernel(xb_ref, x_ref, w1_ref, w2_ref, g_ref, b_ref, y_ref, yb_ref, acc_ref):
    f = pl.program_id(1)

    @pl.when(f == 0)
    def _():
        acc_ref[...] = jnp.zeros_like(acc_ref)

    h = _dot(xb_ref[...], w1_ref[...])
    h = jnp.square(jnp.maximum(h, 0.0)).astype(BF16)
    acc_ref[...] += _dot(h, w2_ref[...])

    @pl.when(f == pl.num_programs(1) - 1)
    def _():
        y = _layer_norm(DN_ALPHA * x_ref[...] + acc_ref[...], g_ref[...], b_ref[...])
        y_ref[...] = y
        yb_ref[...] = y.astype(BF16)


def ffn_residual_ln(x, xb, w1, w2, g, b, tm=512, tf=1024):
    M, D = x.shape
    F = w1.shape[1]
    tm, tf = min(tm, M), min(tf, F)
    assert M % tm == 0 and F % tf == 0
    return pl.pallas_call(
        _ffn_kernel,
        grid=(M // tm, F // tf),
        in_specs=[pl.BlockSpec((tm, D), lambda i, f: (i, 0)),
                  pl.BlockSpec((tm, D), lambda i, f: (i, 0)),
                  pl.BlockSpec((D, tf), lambda i, f: (0, f)),
                  pl.BlockSpec((tf, D), lambda i, f: (f, 0)),
                  pl.BlockSpec((1, D), lambda i, f: (0, 0)),
                  pl.BlockSpec((1, D), lambda i, f: (0, 0))],
        out_specs=[pl.BlockSpec((tm, D), lambda i, f: (i, 0)),
                   pl.BlockSpec((tm, D), lambda i, f: (i, 0))],
        out_shape=[jax.ShapeDtypeStruct((M, D), F32), jax.ShapeDtypeStruct((M, D), BF16)],
        scratch_shapes=[pltpu.VMEM((tm, D), F32)],
        compiler_params=_cparams(("parallel", "arbitrary")),
        name="ffn_ln",
    )(xb, x, w1, w2, g.reshape(1, D), b.reshape(1, D))


def _sb_kernel(q_ref, k_ref, v_ref, o_ref, *, tq, tk, sub):
    i = pl.program_id(2)
    q = q_ref[...]
    r = lax.broadcasted_iota(I32, (sub, sub), 0)
    c = lax.broadcasted_iota(I32, (sub, sub), 1)
    tri = jnp.where(r > c, 1.0, 0.0).astype(BF16)
    qpos = i * tq + lax.broadcasted_iota(I32, (tq, tk), 0)
    koff = lax.broadcasted_iota(I32, (tq, tk), 1)

    def chunk(kc, carry, acc, diag):
        start = pl.multiple_of(kc * tk, tk)
        k = k_ref[pl.ds(start, tk), :]
        v = v_ref[pl.ds(start, tk), :]
        z = _dot_nt(q, k)
        softplus = jnp.log(1.0 + jnp.exp2(-jnp.abs(z))) * LOG2E
        log_beta = jnp.minimum(z, 0.0) - softplus
        log_keep = log_beta - z
        if diag:
            strict = (start + koff) < qpos
            log_keep = jnp.where(strict, log_keep, 0.0)
        laters = []
        for s in reversed(range(tk // sub)):
            lk = log_keep[:, s * sub:(s + 1) * sub]
            later = _dot(lk.astype(BF16), tri) + carry
            laters.append(later)
            carry = later[:, 0:1] + lk[:, 0:1]
        later = jnp.concatenate(laters[::-1], axis=1)
        a = jnp.exp2(log_beta + later)
        if diag:
            a = jnp.where(strict, a, 0.0)
        acc = acc + _dot(a.astype(BF16), v)
        return carry, acc

    kd = (i * tq + tq - 1) // tk
    carry = jnp.zeros((tq, 1), F32)
    acc = jnp.zeros((tq, q.shape[1]), F32)
    carry, acc = chunk(kd, carry, acc, True)
    carry, acc = lax.fori_loop(0, kd, lambda it, st: chunk(kd - 1 - it, st[0], st[1], False),
                               (carry, acc))
    o_ref[...] = acc.astype(o_ref.dtype)


def stick_breaking(h, B, S, q_col, k_col, v_col, n_heads, tq=1024, tk=1024, sub=256):
    dh = HEAD_DIM_SB
    tq, tk = min(tq, S), min(tk, S)
    sub = min(sub, tk)
    assert S % tq == 0 and S % tk == 0 and tk % sub == 0 and tk % tq == 0
    return pl.pallas_call(
        functools.partial(_sb_kernel, tq=tq, tk=tk, sub=sub),
        grid=(B, n_heads, S // tq),
        in_specs=[pl.BlockSpec((None, tq, dh), lambda b, hh, i: (b, i, q_col + hh)),
                  pl.BlockSpec((None, S, dh), lambda b, hh, i: (b, 0, k_col + hh)),
                  pl.BlockSpec((None, S, dh), lambda b, hh, i: (b, 0, v_col + hh))],
        out_specs=pl.BlockSpec((None, tq, dh), lambda b, hh, i: (b, i, hh)),
        out_shape=jax.ShapeDtypeStruct((B, S, n_heads * dh), BF16),
        compiler_params=_cparams(("parallel", "parallel", "arbitrary")),
        name="stick_breaking",
    )(h, h, h)


def _sel_kernel(qi_ref, ki_ref, wi_ref, o_ref, keys_ref, *, blk, nk, ch, topk, w_scale):
    n = pl.program_id(1)
    nh = qi_ref.shape[0]
    cw = ch * blk
    last = n // ch
    qall = qi_ref[...].reshape(nh * blk, LANES).astype(BF16)
    w8 = wi_ref[...].T[:nh, :] * w_scale
    row = lax.broadcasted_iota(I32, (blk, blk), 0)
    col = lax.broadcasted_iota(I32, (blk, blk), 1)
    tri = jnp.where(col < row, 1.0, 0.0).astype(BF16)
    qpos = n * blk + lax.broadcasted_iota(I32, (cw, blk), 1)
    koff = lax.broadcasted_iota(I32, (cw, blk), 0)

    def score_keys(cc, diag):
        start = pl.multiple_of(cc * cw, cw)
        d = _dot_nt(ki_ref[pl.ds(start, cw), :].astype(BF16), qall)
        d = jnp.maximum(d, 0.0)
        sc = d[:, 0:blk] * w8[0:1, :]
        for hh in range(1, nh):
            sc = sc + d[:, hh * blk:(hh + 1) * blk] * w8[hh:hh + 1, :]
        sc = jnp.where(sc == 0.0, 0.0, sc)
        bits = pltpu.bitcast(sc, I32)
        key = jnp.where(bits < 0, bits ^ 0x7FFFFFFF, bits)
        if diag:
            key = jnp.where(start + koff <= qpos, key, INT_MIN)
        keys_ref[pl.ds(cc * ch, ch)] = key.reshape(ch, blk, blk)

    score_keys(last, True)
    lax.fori_loop(0, last, lambda cc, c: (score_keys(cc, False), c)[1], 0)

    def count(pred):
        def body(cc, c):
            kk = keys_ref[pl.ds(cc * ch, ch)]
            for j in range(ch):
                c = c + jnp.where(pred(kk[j]), 1, 0)
            return c
        c = lax.fori_loop(0, last + 1, body, jnp.zeros((blk, blk), I32))
        return jnp.sum(c, axis=0, keepdims=True)

    c0 = count(lambda key: key >= 0)
    theta = jnp.where(c0 >= topk, 0, INT_MIN).astype(I32)

    def bit_step(i, theta):
        cand = theta + lax.shift_left(jnp.int32(1), 30 - i)
        c = count(lambda key: key >= cand)
        return jnp.where(c >= topk, cand, theta)

    theta = lax.fori_loop(0, 31, bit_step, theta)
    need = (topk - count(lambda key: key > theta)).astype(F32)

    def emit(cc, carry, diag):
        kk = keys_ref[pl.ds(cc * ch, ch)]
        for j in range(ch):
            key = kk[j]
            eqf = jnp.where(key == theta, 1.0, 0.0)
            rank = _dot(tri, eqf.astype(BF16)) + carry
            sel = jnp.where(key > theta, 1.0, jnp.where(rank < need, eqf, 0.0))
            if diag:
                kb = cc * ch + j
                sel = jnp.where(kb * blk + row <= n * blk + col, sel, 0.0)
            madd = jnp.where(sel > 0.5, 0.0, -jnp.inf)
            o_ref[cc * ch + j] = madd.T.astype(o_ref.dtype)
            carry = carry + jnp.sum(eqf, axis=0, keepdims=True)
        return carry

    carry = lax.fori_loop(0, last, lambda cc, c: emit(cc, c, False), jnp.zeros((1, blk), F32))
    emit(last, carry, True)

    def fill(kb, c):
        o_ref[kb] = jnp.full((blk, blk), -jnp.inf, o_ref.dtype)
        return c

    lax.fori_loop((last + 1) * ch, nk, fill, 0)


def dsa_select(hi, topk, ch=4):
    B, nslab, S, _ = hi.shape
    nh = N_HEADS_IDX
    assert nslab == nh + 2
    blk = Q_BLOCK
    nq = S // blk
    ch = min(ch, nq)
    assert nq % ch == 0
    return pl.pallas_call(
        functools.partial(_sel_kernel, blk=blk, nk=nq, ch=ch, topk=topk, w_scale=N_HEADS_IDX ** -0.5),
        grid=(B, nq),
        in_specs=[pl.BlockSpec((None, nh, blk, LANES), lambda b, i: (b, 0, i, 0)),
                  pl.BlockSpec((None, None, S, LANES), lambda b, i: (b, nh, 0, 0)),
                  pl.BlockSpec((None, None, blk, LANES), lambda b, i: (b, nh + 1, i, 0))],
        out_specs=pl.BlockSpec((None, None, nq, blk, blk), lambda b, i: (b, i, 0, 0, 0)),
        out_shape=jax.ShapeDtypeStruct((B, nq, nq, blk, blk), BF16),
        scratch_shapes=[pltpu.VMEM((nq, blk, blk), I32)],
        compiler_params=_cparams(("parallel", "arbitrary")),
        name="dsa_select",
    )(hi, hi, hi)


def _dsa_kernel(q_ref, k_ref, v_ref, m_ref, vec_ref, o_ref, bias_ref, *, tq, tk, dh):
    i = pl.program_id(2)
    blk = m_ref.shape[-1]
    rq, rk = tq // blk, tk // blk
    nb = bias_ref.shape[1] - 1

    @pl.when((i == 0) & (pl.program_id(1) == 0))
    def _():
        def expand(kk, c):
            for hh in range(2):
                rows = jnp.broadcast_to(vec_ref[hh, pl.ds(kk, 1), :], (blk, 2 * blk))
                bias_ref[hh, kk] = pltpu.roll(rows, 0, 1, stride=1, stride_axis=0)[:, :blk]
            return c
        lax.fori_loop(0, nb + 1, expand, 0)

    q = q_ref[...]
    lane = lax.broadcasted_iota(I32, q.shape, 1)
    halves = (lane < dh, lane >= dh)
    qs = [jnp.where(hm, q, jnp.zeros_like(q)) for hm in halves]
    vlane = lax.broadcasted_iota(I32, (tk, LANES), 1)
    vhalves = (vlane < dh, vlane >= dh)

    def tiles(load):
        return jnp.concatenate(
            [jnp.concatenate([load(r, c) for c in range(rk)], axis=1) for r in range(rq)], axis=0)

    def body(kc, carry):
        start = pl.multiple_of(kc * tk, tk)
        k = k_ref[pl.ds(start, tk), :]
        v = v_ref[pl.ds(start, tk), :]
        madd = tiles(lambda r, c: m_ref[r, kc * rk + c]).astype(F32)
        out = []
        for hh in range(2):
            m, acc = carry[hh]
            bias = tiles(lambda r, c: bias_ref[hh, jnp.minimum(nb - (i * rq + r) + (kc * rk + c), nb)])
            s = _dot_nt(qs[hh], k) + bias + madd
            m_new = jnp.maximum(m, jnp.max(s, axis=1, keepdims=True))
            p = jnp.exp2(s - m_new)
            vh = jnp.where(vhalves[hh], v, jnp.ones_like(v))
            acc = acc * jnp.exp2(m - m_new) + _dot(p.astype(BF16), vh)
            out.append((m_new, acc))
        return tuple(out)

    init = tuple((jnp.full((tq, 1), NEG, F32), jnp.zeros((tq, LANES), F32)) for _ in range(2))
    n_chunks = (i * tq + tq - 1) // tk + 1
    (_, a0), (_, a1) = lax.fori_loop(0, n_chunks, body, init)
    o_ref[...] = jnp.where(halves[0], a0 / a0[:, dh:dh + 1], a1 / a1[:, 0:1]).astype(o_ref.dtype)


def dsa_attention(h, madd, bias_vecs, B, S, q_col, k_col, v_col, tq=1024, tk=1024):
    blk = Q_BLOCK
    nq = S // blk
    tq, tk = min(tq, S), min(tk, S)
    assert S % tq == 0 and S % tk == 0 and tq % blk == 0 and tk % blk == 0
    n_pairs = N_HEADS_DSA * HEAD_DIM_DSA // LANES
    assert bias_vecs.shape == (2 * n_pairs, nq + 1, 2 * blk)
    return pl.pallas_call(
        functools.partial(_dsa_kernel, tq=tq, tk=tk, dh=HEAD_DIM_DSA),
        grid=(n_pairs, B, S // tq),
        in_specs=[pl.BlockSpec((None, tq, LANES), lambda p, b, i: (b, i, q_col + p)),
                  pl.BlockSpec((None, S, LANES), lambda p, b, i: (b, 0, k_col + p)),
                  pl.BlockSpec((None, S, LANES), lambda p, b, i: (b, 0, v_col + p)),
                  pl.BlockSpec((None, tq // blk, nq, blk, blk), lambda p, b, i: (b, i, 0, 0, 0)),
                  pl.BlockSpec((2, nq + 1, 2 * blk), lambda p, b, i: (p, 0, 0))],
        out_specs=pl.BlockSpec((None, tq, LANES), lambda p, b, i: (b, i, p)),
        out_shape=jax.ShapeDtypeStruct((B, S, n_pairs * LANES), BF16),
        scratch_shapes=[pltpu.VMEM((2, nq + 1, blk, blk), F32)],
        compiler_params=_cparams(("parallel", "arbitrary", "arbitrary")),
        name="dsa_attention",
    )(h, h, h, madd, bias_vecs)


def _dil_kernel(q_ref, k_ref, v_ref, bias_ref, o_ref, qf, kf, vf, m_s, l_s, acc_s,
                *, blk, chunk, dils, unroll):
    c = pl.program_id(2)
    base = pl.multiple_of(c * chunk, chunk)

    @pl.when(c == 0)
    def _():
        qf[...] = q_ref[...].astype(F32)
        kf[...] = k_ref[...].astype(F32)
        vf[...] = v_ref[...].astype(F32)

    for g, d in enumerate(dils):
        span = blk * d

        def group(it, carry, g=g, d=d, span=span):
            for jj in range(unroll):
                t = it * unroll + jj
                u = t // d
                r = t - u * d
                start = base + u * span + r
                has_prev = start >= span
                prev = jnp.where(has_prev, start - span, start)
                if d == 1:
                    start = pl.multiple_of(start, blk)
                    prev = pl.multiple_of(prev, blk)
                    rows, prows = pl.ds(start, blk), pl.ds(prev, blk)
                    lrows = pl.ds(pl.multiple_of(start - base, blk), blk)
                else:
                    rows, prows = pl.ds(start, blk, stride=d), pl.ds(prev, blk, stride=d)
                    lrows = pl.ds(start - base, blk, stride=d)
                q = qf[rows, :].astype(BF16)
                kk = jnp.concatenate([kf[prows, :], kf[rows, :]], axis=0).astype(BF16)
                vv = jnp.concatenate([vf[prows, :], vf[rows, :]], axis=0).astype(BF16)
                s = _dot_nt(q, kk) + bias_ref[g, jnp.where(has_prev, 0, 1)]
                m_t = jnp.max(s, axis=1, keepdims=True)
                e = jnp.exp2(s - m_t)
                l_t = jnp.sum(e, axis=1, keepdims=True)
                m_s[g, lrows, :] = jnp.broadcast_to(m_t, (blk, LANES))
                l_s[g, lrows, :] = jnp.broadcast_to(l_t, (blk, LANES))
                acc_s[g, lrows, :] = _dot(e.astype(BF16), vv)
            return carry

        lax.fori_loop(0, chunk // (blk * unroll), group, 0)

    m = m_s[0]
    for g in range(1, len(dils)):
        m = jnp.maximum(m, m_s[g])
    num = den = None
    for g in range(len(dils)):
        w = jnp.exp2(m_s[g] - m)
        num = acc_s[g] * w if num is None else num + acc_s[g] * w
        den = l_s[g] * w if den is None else den + l_s[g] * w
    o_ref[...] = (num / den).astype(o_ref.dtype)


def dilated_attention(h, bias_tiles, B, S, q_col, k_col, v_col, n_heads, unroll=16):
    blk = Q_BLOCK
    dils = tuple(d for _, d in DILATED_CONFIGS)
    assert all(w // d == blk for w, d in DILATED_CONFIGS)
    chunk = blk * max(dils)
    assert S % chunk == 0 and (chunk // blk) % unroll == 0
    dh = HEAD_DIM_DIL
    ng = len(dils)
    return pl.pallas_call(
        functools.partial(_dil_kernel, blk=blk, chunk=chunk, dils=dils, unroll=unroll),
        grid=(B, n_heads, S // chunk),
        in_specs=[pl.BlockSpec((None, S, dh), lambda b, hh, c: (b, 0, q_col + hh)),
                  pl.BlockSpec((None, S, dh), lambda b, hh, c: (b, 0, k_col + hh)),
                  pl.BlockSpec((None, S, dh), lambda b, hh, c: (b, 0, v_col + hh)),
                  pl.BlockSpec((None, ng, 2, blk, 2 * blk), lambda b, hh, c: (hh, 0, 0, 0, 0))],
        out_specs=pl.BlockSpec((None, chunk, dh), lambda b, hh, c: (b, c, hh)),
        out_shape=jax.ShapeDtypeStruct((B, S, n_heads * dh), BF16),
        scratch_shapes=[pltpu.VMEM((S, dh), F32), pltpu.VMEM((S, dh), F32), pltpu.VMEM((S, dh), F32),
                        pltpu.VMEM((ng, chunk, LANES), F32), pltpu.VMEM((ng, chunk, LANES), F32),
                        pltpu.VMEM((ng, chunk, dh), F32)],
        compiler_params=_cparams(("parallel", "parallel", "arbitrary")),
        name="dilated_attention",
    )(h, h, h, bias_tiles)


def _rel_bucket(dist):
    max_exact = N_BUCKETS // 2
    d_f = jnp.maximum(dist, 1).astype(F32)
    large = max_exact + (jnp.log(d_f / max_exact) / math.log(BUCKET_MAX_DIST / max_exact)
                         * (N_BUCKETS - max_exact)).astype(I32)
    large = jnp.minimum(large, N_BUCKETS - 1)
    return jnp.where(dist < max_exact, dist, large)


def _bias_by_distance(rel_bias, n):
    return rel_bias.astype(F32)[_rel_bucket(jnp.arange(n, dtype=I32))].T


def _toeplitz_band(g, nblk):
    blk = Q_BLOCK
    period = blk * (nblk + 2)
    m = jnp.arange(period, dtype=I32)
    m = jnp.where(m >= period - blk, m - period, m)
    v = g[:, jnp.clip(blk * nblk - m, 0, g.shape[1] - 1)]
    flat = jnp.tile(v, (1, blk + 1))[:, :blk * (period - 1)]
    return flat.reshape(g.shape[0], blk, period - 1)[:, :, :blk * (nblk + 1)]


def _causal_bias_vectors(rel_bias, nq):
    blk = Q_BLOCK
    g = _bias_by_distance(rel_bias, nq * blk) * LOG2E
    m = np.arange(2 * blk)
    m = np.where(m >= blk, m - 2 * blk, m)
    dist = blk * (nq - np.arange(nq + 1))[:, None] - m[None, :]
    return g[:, np.clip(dist, 0, nq * blk - 1)]


def _dilated_bias_tiles(rel_bias):
    blk = Q_BLOCK
    g = _bias_by_distance(rel_bias, 2 * blk * max(d for _, d in DILATED_CONFIGS)) * LOG2E
    j = blk + np.arange(blk)[:, None] - np.arange(2 * blk)[None, :]
    band = (j >= 0) & (j <= blk)
    masks = np.stack([band, band & (np.arange(2 * blk)[None, :] >= blk)])
    tiles = jnp.stack([_toeplitz_band(g[:, ::d][:, :2 * blk], 1) for _, d in DILATED_CONFIGS], axis=1)
    return jnp.where(masks[None, None], tiles[:, :, None], NEG)


def _lane_slabs(w, width):
    K, N = w.shape
    w = w.reshape(K, N // width, width)
    return jnp.pad(w, ((0, 0), (0, 0), (0, LANES - width))).reshape(K, N // width * LANES)


def even_mixer(x, w_in, w_out, rel_bias, B, S):
    half = N_HEADS_SB * HEAD_DIM_SB
    n_main = 6 * half
    n_qi = N_HEADS_IDX * HEAD_DIM_IDX
    slab = half // LANES
    col_scale = _col_scale(n_main, [(0, half, HEAD_DIM_SB ** -0.5 * LOG2E),
                                    (3 * half, 4 * half, HEAD_DIM_DSA ** -0.5 * LOG2E)])
    h = matmul(x, w_in[:, :n_main].astype(BF16), col_scale, BF16).reshape(B, S, n_main)
    w_idx = w_in[:, n_main:]
    w_idx = jnp.concatenate([_lane_slabs(w_idx[:, :n_qi], HEAD_DIM_IDX),
                             _lane_slabs(w_idx[:, n_qi:n_qi + HEAD_DIM_IDX], HEAD_DIM_IDX),
                             _lane_slabs(w_idx[:, n_qi + HEAD_DIM_IDX:], N_HEADS_IDX)], axis=1)
    idx_scale = _col_scale(w_idx.shape[1], [(0, N_HEADS_IDX * LANES, HEAD_DIM_IDX ** -0.5)])
    hi = matmul_slabs(x, w_idx.astype(BF16), idx_scale, B, S)
    madd = dsa_select(hi, min(TOPK_MAX, S // 4))
    oa = stick_breaking(h, B, S, 0, slab, 2 * slab, N_HEADS_SB)
    ob = dsa_attention(h, madd, _causal_bias_vectors(rel_bias, S // Q_BLOCK), B, S,
                       3 * slab, 4 * slab, 5 * slab)
    w_out = w_out.astype(BF16)
    return [oa.reshape(B * S, half), ob.reshape(B * S, half)], [w_out[:half], w_out[half:]]


def odd_mixer(xb, w_in, w_out, rel_bias, B, S):
    width = N_HEADS_DIL * HEAD_DIM_DIL
    col_scale = _col_scale(3 * width, [(0, width, HEAD_DIM_DIL ** -0.5 * LOG2E)])
    h = matmul(xb, w_in.astype(BF16), col_scale, BF16).reshape(B, S, 3 * width)
    o = dilated_attention(h, _dilated_bias_tiles(rel_bias), B, S, 0, N_HEADS_DIL, 2 * N_HEADS_DIL,
                          N_HEADS_DIL)
    return [o.reshape(B * S, width)], [w_out.astype(BF16)]


def kernel(x, even_w_in, even_w_out, odd_w_in, odd_w_out, rel_bias, ln_mix_g, ln_mix_b,
           ffn_w1, ffn_w2, ln_ffn_g, ln_ffn_b):
    B, S, D = x.shape
    x2 = x.reshape(B * S, D)
    xb = x2
    for layer in range(DEPTH):
        if layer % 2 == 0:
            a_list, w_list = even_mixer(xb, even_w_in[layer // 2], even_w_out[layer // 2],
                                        rel_bias, B, S)
        else:
            a_list, w_list = odd_mixer(xb, odd_w_in[layer // 2], odd_w_out[layer // 2],
                                       rel_bias, B, S)
        x2, xb = proj_residual_ln(a_list, w_list, x2, ln_mix_g[layer], ln_mix_b[layer])
        x2, xb = ffn_residual_ln(x2, xb, ffn_w1[layer].astype(BF16), ffn_w2[layer].astype(BF16),
                                 ln_ffn_g[layer], ln_ffn_b[layer])
    return x2.reshape(B, S, D)
```

```python
import functools
import math

import jax
import jax.numpy as jnp
import numpy as np
from jax import lax
from jax.experimental import pallas as pl
from jax.experimental.pallas import tpu as pltpu

F32 = jnp.float32
BF16 = jnp.bfloat16
I32 = jnp.int32
I16 = jnp.int16

Q_BLOCK = 128
HEAD_DIM_SB = 128
N_HEADS_SB = 8
N_HEADS_DSA = 16
HEAD_DIM_DSA = 64
N_HEADS_IDX = 8
HEAD_DIM_IDX = 64
TOPK_MAX = 256
N_HEADS_DIL = 16
HEAD_DIM_DIL = 128
DILATED_CONFIGS = ((128, 1), (512, 4), (2048, 16))
N_BUCKETS = 32
BUCKET_MAX_DIST = 2048
DEPTH = 2
DN_ALPHA = (2 * DEPTH) ** 0.25
LN_EPS = 1e-5
NEG = -1e30
INT_MIN = -(2 ** 31)
INT16_MIN = -(2 ** 15)
LOG2E = math.log2(math.e)

LANES = 128
VMEM_LIMIT_BYTES = 52 * 1024 * 1024


def _cparams(sem):
    return pltpu.CompilerParams(dimension_semantics=sem, vmem_limit_bytes=VMEM_LIMIT_BYTES)


def _dot(a, b):
    return jnp.dot(a, b, preferred_element_type=F32)


def _dot_nt(a, b):
    return lax.dot_general(a, b, (((1,), (1,)), ((), ())), preferred_element_type=F32)


def _mm_kernel(a_ref, b_ref, s_ref, o_ref):
    acc = _dot(a_ref[...].astype(BF16), b_ref[...])
    o_ref[...] = (acc * s_ref[...]).astype(o_ref.dtype)


def matmul(a, b, col_scale, out_dtype, tm=1024, tn=1024):
    M, K = a.shape
    N = b.shape[1]
    tm, tn = min(tm, M), min(tn, N)
    assert M % tm == 0 and N % tn == 0
    return pl.pallas_call(
        _mm_kernel,
        grid=(M // tm, N // tn),
        in_specs=[pl.BlockSpec((tm, K), lambda i, j: (i, 0)),
                  pl.BlockSpec((K, tn), lambda i, j: (0, j)),
                  pl.BlockSpec((1, tn), lambda i, j: (0, j))],
        out_specs=pl.BlockSpec((tm, tn), lambda i, j: (i, j)),
        out_shape=jax.ShapeDtypeStruct((M, N), out_dtype),
        compiler_params=_cparams(("parallel", "arbitrary")),
        name="proj_matmul",
    )(a, b, col_scale.reshape(1, N))


def _col_scale(n, scaled):
    s = np.ones((n,), np.float32)
    for lo, hi, f in scaled:
        s[lo:hi] = f
    return jnp.asarray(s)


def _mm_slab_kernel(a_ref, b_ref, s_ref, o_ref):
    acc = _dot(a_ref[...].astype(BF16), b_ref[...]) * s_ref[...]
    for j in range(o_ref.shape[0]):
        o_ref[j] = acc[:, j * LANES:(j + 1) * LANES]


def matmul_slabs(a, b, col_scale, B, S, tm=1024, tn=1280):
    M, K = a.shape
    N = b.shape[1]
    tm, tn = min(tm, S), min(tn, N)
    assert M == B * S and S % tm == 0 and N % tn == 0 and tn % LANES == 0
    steps = S // tm
    return pl.pallas_call(
        _mm_slab_kernel,
        grid=(B, steps, N // tn),
        in_specs=[pl.BlockSpec((tm, K), lambda b, i, j: (b * steps + i, 0)),
                  pl.BlockSpec((K, tn), lambda b, i, j: (0, j)),
                  pl.BlockSpec((1, tn), lambda b, i, j: (0, j))],
        out_specs=pl.BlockSpec((None, tn // LANES, tm, LANES), lambda b, i, j: (b, j, i, 0)),
        out_shape=jax.ShapeDtypeStruct((B, N // LANES, S, LANES), F32),
        compiler_params=_cparams(("parallel", "parallel", "arbitrary")),
        name="indexer_proj",
    )(a, b, col_scale.reshape(1, N))


def _layer_norm(y, g, b):
    mu = jnp.mean(y, axis=-1, keepdims=True)
    yc = y - mu
    var = jnp.mean(jnp.square(yc), axis=-1, keepdims=True)
    return yc * lax.rsqrt(var + LN_EPS) * g + b


def _proj_ln_kernel(*refs, n_in):
    a_refs, w_refs = refs[:n_in], refs[n_in:2 * n_in]
    x_ref, g_ref, b_ref, y_ref, yb_ref = refs[2 * n_in:]
    acc = _dot(a_refs[0][...], w_refs[0][...])
    for a_ref, w_ref in zip(a_refs[1:], w_refs[1:]):
        acc = acc + _dot(a_ref[...], w_ref[...])
    y = _layer_norm(DN_ALPHA * x_ref[...] + acc, g_ref[...], b_ref[...])
    y_ref[...] = y
    yb_ref[...] = y.astype(BF16)


def proj_residual_ln(a_list, w_list, x, g, b, tm=512):
    M, D = x.shape
    tm = min(tm, M)
    assert M % tm == 0
    n_in = len(a_list)
    in_specs = ([pl.BlockSpec((tm, a.shape[1]), lambda i: (i, 0)) for a in a_list]
                + [pl.BlockSpec(w.shape, lambda i: (0, 0)) for w in w_list]
                + [pl.BlockSpec((tm, D), lambda i: (i, 0)),
                   pl.BlockSpec((1, D), lambda i: (0, 0)),
                   pl.BlockSpec((1, D), lambda i: (0, 0))])
    return pl.pallas_call(
        functools.partial(_proj_ln_kernel, n_in=n_in),
        grid=(M // tm,),
        in_specs=in_specs,
        out_specs=[pl.BlockSpec((tm, D), lambda i: (i, 0)),
                   pl.BlockSpec((tm, D), lambda i: (i, 0))],
        out_shape=[jax.ShapeDtypeStruct((M, D), F32), jax.ShapeDtypeStruct((M, D), BF16)],
        compiler_params=_cparams(("parallel",)),
        name="out_proj_ln",
    )(*a_list, *w_list, x, g.reshape(1, D), b.reshape(1, D))


def _ffn_kernel(xb_ref, x_ref, w1_ref, w2_ref, g_ref, b_ref, y_ref, yb_ref, acc_ref):
    f = pl.program_id(1)

    @pl.when(f == 0)
    def _():
        acc_ref[...] = jnp.zeros_like(acc_ref)

    h = _dot(xb_ref[...], w1_ref[...])
    h = jnp.square(jnp.maximum(h, 0.0)).astype(BF16)
    acc_ref[...] += _dot(h, w2_ref[...])

    @pl.when(f == pl.num_programs(1) - 1)
    def _():
        y = _layer_norm(DN_ALPHA * x_ref[...] + acc_ref[...], g_ref[...], b_ref[...])
        y_ref[...] = y
        yb_ref[...] = y.astype(BF16)


def ffn_residual_ln(x, xb, w1, w2, g, b, tm=512, tf=1024):
    M, D = x.shape
    F = w1.shape[1]
    tm, tf = min(tm, M), min(tf, F)
    assert M % tm == 0 and F % tf == 0
    return pl.pallas_call(
        _ffn_kernel,
        grid=(M // tm, F // tf),
        in_specs=[pl.BlockSpec((tm, D), lambda i, f: (i, 0)),
                  pl.BlockSpec((tm, D), lambda i, f: (i, 0)),
                  pl.BlockSpec((D, tf), lambda i, f: (0, f)),
                  pl.BlockSpec((tf, D), lambda i, f: (f, 0)),
                  pl.BlockSpec((1, D), lambda i, f: (0, 0)),
                  pl.BlockSpec((1, D), lambda i, f: (0, 0))],
        out_specs=[pl.BlockSpec((tm, D), lambda i, f: (i, 0)),
                   pl.BlockSpec((tm, D), lambda i, f: (i, 0))],
        out_shape=[jax.ShapeDtypeStruct((M, D), F32), jax.ShapeDtypeStruct((M, D), BF16)],
        scratch_shapes=[pltpu.VMEM((tm, D), F32)],
        compiler_params=_cparams(("parallel", "arbitrary")),
        name="ffn_ln",
    )(xb, x, w1, w2, g.reshape(1, D), b.reshape(1, D))


def _sb_kernel(q_ref, k_ref, v_ref, o_ref, *, tq, tk, sub):
    i = pl.program_id(2)
    q = q_ref[...]
    r = lax.broadcasted_iota(I32, (sub, sub), 0)
    c = lax.broadcasted_iota(I32, (sub, sub), 1)
    tri = jnp.where(r > c, 1.0, 0.0).astype(BF16)
    qpos = i * tq + lax.broadcasted_iota(I32, (tq, tk), 0)
    koff = lax.broadcasted_iota(I32, (tq, tk), 1)

    def chunk(kc, carry, acc, diag):
        start = pl.multiple_of(kc * tk, tk)
        k = k_ref[pl.ds(start, tk), :]
        v = v_ref[pl.ds(start, tk), :]
        z = _dot_nt(q, k)
        softplus = jnp.log(1.0 + jnp.exp2(-jnp.abs(z))) * LOG2E
        log_beta = jnp.minimum(z, 0.0) - softplus
        log_keep = log_beta - z
        if diag:
            strict = (start + koff) < qpos
            log_keep = jnp.where(strict, log_keep, 0.0)
        laters = []
        for s in reversed(range(tk // sub)):
            lk = log_keep[:, s * sub:(s + 1) * sub]
            later = _dot(lk.astype(BF16), tri) + carry
            laters.append(later)
            carry = later[:, 0:1] + lk[:, 0:1]
        later = jnp.concatenate(laters[::-1], axis=1)
        a = jnp.exp2(log_beta + later)
        if diag:
            a = jnp.where(strict, a, 0.0)
        acc = acc + _dot(a.astype(BF16), v)
        return carry, acc

    kd = (i * tq + tq - 1) // tk
    carry = jnp.zeros((tq, 1), F32)
    acc = jnp.zeros((tq, q.shape[1]), F32)
    carry, acc = chunk(kd, carry, acc, True)
    carry, acc = lax.fori_loop(0, kd, lambda it, st: chunk(kd - 1 - it, st[0], st[1], False),
                               (carry, acc))
    o_ref[...] = acc.astype(o_ref.dtype)


def stick_breaking(h, B, S, q_col, k_col, v_col, n_heads, tq=1024, tk=1024, sub=256):
    dh = HEAD_DIM_SB
    tq, tk = min(tq, S), min(tk, S)
    sub = min(sub, tk)
    assert S % tq == 0 and S % tk == 0 and tk % sub == 0 and tk % tq == 0
    return pl.pallas_call(
        functools.partial(_sb_kernel, tq=tq, tk=tk, sub=sub),
        grid=(B, n_heads, S // tq),
        in_specs=[pl.BlockSpec((None, tq, dh), lambda b, hh, i: (b, i, q_col + hh)),
                  pl.BlockSpec((None, S, dh), lambda b, hh, i: (b, 0, k_col + hh)),
                  pl.BlockSpec((None, S, dh), lambda b, hh, i: (b, 0, v_col + hh))],
        out_specs=pl.BlockSpec((None, tq, dh), lambda b, hh, i: (b, i, hh)),
        out_shape=jax.ShapeDtypeStruct((B, S, n_heads * dh), BF16),
        compiler_params=_cparams(("parallel", "parallel", "arbitrary")),
        name="stick_breaking",
    )(h, h, h)


def _sel_kernel(qi_ref, ki_ref, wi_ref, o_ref, keys_ref, hi_ref, lo_ref, *, blk, nk, ch, topk, w_scale):
    n = pl.program_id(1)
    nh = qi_ref.shape[0]
    cw = ch * blk
    last = n // ch
    qall = qi_ref[...].reshape(nh * blk, LANES).astype(BF16)
    w8 = wi_ref[...].T[:nh, :] * w_scale
    row = lax.broadcasted_iota(I32, (blk, blk), 0)
    col = lax.broadcasted_iota(I32, (blk, blk), 1)
    tri = jnp.where(col < row, 1.0, 0.0).astype(BF16)
    qpos = n * blk + lax.broadcasted_iota(I32, (cw, blk), 1)
    koff = lax.broadcasted_iota(I32, (cw, blk), 0)

    def score_keys(cc, diag):
        start = pl.multiple_of(cc * cw, cw)
        d = _dot_nt(ki_ref[pl.ds(start, cw), :].astype(BF16), qall)
        d = jnp.maximum(d, 0.0)
        sc = d[:, 0:blk] * w8[0:1, :]
        for hh in range(1, nh):
            sc = sc + d[:, hh * blk:(hh + 1) * blk] * w8[hh:hh + 1, :]
        sc = jnp.where(sc == 0.0, 0.0, sc)
        bits = pltpu.bitcast(sc, I32)
        key = jnp.where(bits < 0, bits ^ 0x7FFFFFFF, bits)
        if diag:
            key = jnp.where(start + koff <= qpos, key, INT_MIN)
        keys_ref[pl.ds(cc * ch, ch)] = key.reshape(ch, blk, blk)
        hi_ref[pl.ds(cc * ch, ch)] = lax.shift_right_arithmetic(key, 16).astype(I16).reshape(ch, blk, blk)

    score_keys(last, True)
    lax.fori_loop(0, last, lambda cc, c: (score_keys(cc, False), c)[1], 0)

    def count(ref, one, pred):
        def body(cc, c):
            kk = ref[pl.ds(cc * ch, ch)]
            for j in range(ch):
                c = c + jnp.where(pred(kk[j]), one, one - one)
            return c
        c = lax.fori_loop(0, last + 1, body, jnp.zeros((blk, blk), one.dtype))
        return jnp.sum(c.astype(I32), axis=0, keepdims=True)

    def kth_largest16(ref, k):
        one = jnp.int16(1)
        c0 = count(ref, one, lambda h: h >= jnp.int16(0))
        th = jnp.where(c0 >= k, 0, INT16_MIN).astype(I32)

        def bit_step(i, th):
            cand = th + lax.shift_left(jnp.int32(1), 14 - i)
            c = count(ref, one, lambda h: h >= cand.astype(I16))
            return jnp.where(c >= k, cand, th)

        return lax.fori_loop(0, 15, bit_step, th)

    th_hi = kth_largest16(hi_ref, topk)
    th_hi16 = th_hi.astype(I16)
    k_lo = topk - count(hi_ref, jnp.int16(1), lambda h: h > th_hi16)

    def stage_low(cc, c):
        key = keys_ref[pl.ds(cc * ch, ch)]
        low = ((key & 0xFFFF) + INT16_MIN).astype(I16)
        lo_ref[pl.ds(cc * ch, ch)] = jnp.where(hi_ref[pl.ds(cc * ch, ch)] == th_hi16, low, jnp.int16(INT16_MIN))
        return c

    lax.fori_loop(0, last + 1, stage_low, 0)
    th_lo = kth_largest16(lo_ref, k_lo)
    theta = lax.shift_left(th_hi, 16) + (th_lo - INT16_MIN)
    need = (topk - count(keys_ref, jnp.int32(1), lambda key: key > theta)).astype(F32)

    def emit(cc, carry, diag):
        kk = keys_ref[pl.ds(cc * ch, ch)]
        for j in range(ch):
            key = kk[j]
            eqf = jnp.where(key == theta, 1.0, 0.0)
            rank = _dot(tri, eqf.astype(BF16)) + carry
            sel = jnp.where(key > theta, 1.0, jnp.where(rank < need, eqf, 0.0))
            if diag:
                kb = cc * ch + j
                sel = jnp.where(kb * blk + row <= n * blk + col, sel, 0.0)
            madd = jnp.where(sel > 0.5, 0.0, -jnp.inf)
            o_ref[cc * ch + j] = madd.T.astype(o_ref.dtype)
            carry = carry + jnp.sum(eqf, axis=0, keepdims=True)
        return carry

    carry = lax.fori_loop(0, last, lambda cc, c: emit(cc, c, False), jnp.zeros((1, blk), F32))
    emit(last, carry, True)

    def fill(kb, c):
        o_ref[kb] = jnp.full((blk, blk), -jnp.inf, o_ref.dtype)
        return c

    lax.fori_loop((last + 1) * ch, nk, fill, 0)


def dsa_select(hi, topk, ch=4):
    B, nslab, S, _ = hi.shape
    nh = N_HEADS_IDX
    assert nslab == nh + 2
    blk = Q_BLOCK
    nq = S // blk
    ch = min(ch, nq)
    assert nq % ch == 0
    return pl.pallas_call(
        functools.partial(_sel_kernel, blk=blk, nk=nq, ch=ch, topk=topk, w_scale=N_HEADS_IDX ** -0.5),
        grid=(B, nq),
        in_specs=[pl.BlockSpec((None, nh, blk, LANES), lambda b, i: (b, 0, i, 0)),
                  pl.BlockSpec((None, None, S, LANES), lambda b, i: (b, nh, 0, 0)),
                  pl.BlockSpec((None, None, blk, LANES), lambda b, i: (b, nh + 1, i, 0))],
        out_specs=pl.BlockSpec((None, None, nq, blk, blk), lambda b, i: (b, i, 0, 0, 0)),
        out_shape=jax.ShapeDtypeStruct((B, nq, nq, blk, blk), BF16),
        scratch_shapes=[pltpu.VMEM((nq, blk, blk), I32), pltpu.VMEM((nq, blk, blk), I16),
                        pltpu.VMEM((nq, blk, blk), I16)],
        compiler_params=_cparams(("parallel", "arbitrary")),
        name="dsa_select",
    )(hi, hi, hi)


def _dsa_kernel(q_ref, k_ref, v_ref, m_ref, vec_ref, o_ref, bias_ref, *, tq, tk, dh):
    i = pl.program_id(2)
    blk = m_ref.shape[-1]
    rq, rk = tq // blk, tk // blk
    nb = bias_ref.shape[1] - 1

    @pl.when((i == 0) & (pl.program_id(1) == 0))
    def _():
        def expand(kk, c):
            for hh in range(2):
                rows = jnp.broadcast_to(vec_ref[hh, pl.ds(kk, 1), :], (blk, 2 * blk))
                bias_ref[hh, kk] = pltpu.roll(rows, 0, 1, stride=1, stride_axis=0)[:, :blk]
            return c
        lax.fori_loop(0, nb + 1, expand, 0)

    q = q_ref[...]
    lane = lax.broadcasted_iota(I32, q.shape, 1)
    halves = (lane < dh, lane >= dh)
    qs = [jnp.where(hm, q, jnp.zeros_like(q)) for hm in halves]
    vlane = lax.broadcasted_iota(I32, (tk, LANES), 1)
    vhalves = (vlane < dh, vlane >= dh)

    def tiles(load):
        return jnp.concatenate(
            [jnp.concatenate([load(r, c) for c in range(rk)], axis=1) for r in range(rq)], axis=0)

    def body(kc, carry):
        start = pl.multiple_of(kc * tk, tk)
        k = k_ref[pl.ds(start, tk), :]
        v = v_ref[pl.ds(start, tk), :]
        madd = tiles(lambda r, c: m_ref[r, kc * rk + c]).astype(F32)
        out = []
        for hh in range(2):
            m, acc = carry[hh]
            bias = tiles(lambda r, c: bias_ref[hh, jnp.minimum(nb - (i * rq + r) + (kc * rk + c), nb)])
            s = _dot_nt(qs[hh], k) + bias + madd
            m_new = jnp.maximum(m, jnp.max(s, axis=1, keepdims=True))
            p = jnp.exp2(s - m_new)
            vh = jnp.where(vhalves[hh], v, jnp.ones_like(v))
            acc = acc * jnp.exp2(m - m_new) + _dot(p.astype(BF16), vh)
            out.append((m_new, acc))
        return tuple(out)

    init = tuple((jnp.full((tq, 1), NEG, F32), jnp.zeros((tq, LANES), F32)) for _ in range(2))
    n_chunks = (i * tq + tq - 1) // tk + 1
    (_, a0), (_, a1) = lax.fori_loop(0, n_chunks, body, init)
    o_ref[...] = jnp.where(halves[0], a0 / a0[:, dh:dh + 1], a1 / a1[:, 0:1]).astype(o_ref.dtype)


def dsa_attention(h, madd, bias_vecs, B, S, q_col, k_col, v_col, tq=1024, tk=1024):
    blk = Q_BLOCK
    nq = S // blk
    tq, tk = min(tq, S), min(tk, S)
    assert S % tq == 0 and S % tk == 0 and tq % blk == 0 and tk % blk == 0
    n_pairs = N_HEADS_DSA * HEAD_DIM_DSA // LANES
    assert bias_vecs.shape == (2 * n_pairs, nq + 1, 2 * blk)
    return pl.pallas_call(
        functools.partial(_dsa_kernel, tq=tq, tk=tk, dh=HEAD_DIM_DSA),
        grid=(n_pairs, B, S // tq),
        in_specs=[pl.BlockSpec((None, tq, LANES), lambda p, b, i: (b, i, q_col + p)),
                  pl.BlockSpec((None, S, LANES), lambda p, b, i: (b, 0, k_col + p)),
                  pl.BlockSpec((None, S, LANES), lambda p, b, i: (b, 0, v_col + p)),
                  pl.BlockSpec((None, tq // blk, nq, blk, blk), lambda p, b, i: (b, i, 0, 0, 0)),
                  pl.BlockSpec((2, nq + 1, 2 * blk), lambda p, b, i: (p, 0, 0))],
        out_specs=pl.BlockSpec((None, tq, LANES), lambda p, b, i: (b, i, p)),
        out_shape=jax.ShapeDtypeStruct((B, S, n_pairs * LANES), BF16),
        scratch_shapes=[pltpu.VMEM((2, nq + 1, blk, blk), F32)],
        compiler_params=_cparams(("parallel", "arbitrary", "arbitrary")),
        name="dsa_attention",
    )(h, h, h, madd, bias_vecs)


def _dil_kernel(q_ref, k_ref, v_ref, bias_ref, o_ref, qf, kf, vf, m_s, l_s, acc_s,
                *, blk, chunk, dils, unroll):
    c = pl.program_id(2)
    base = pl.multiple_of(c * chunk, chunk)

    @pl.when(c == 0)
    def _():
        qf[...] = q_ref[...].astype(F32)
        kf[...] = k_ref[...].astype(F32)
        vf[...] = v_ref[...].astype(F32)

    for g, d in enumerate(dils):
        span = blk * d

        def group(it, carry, g=g, d=d, span=span):
            for jj in range(unroll):
                t = it * unroll + jj
                u = t // d
                r = t - u * d
                start = base + u * span + r
                has_prev = start >= span
                prev = jnp.where(has_prev, start - span, start)
                if d == 1:
                    start = pl.multiple_of(start, blk)
                    prev = pl.multiple_of(prev, blk)
                    rows, prows = pl.ds(start, blk), pl.ds(prev, blk)
                    lrows = pl.ds(pl.multiple_of(start - base, blk), blk)
                else:
                    rows, prows = pl.ds(start, blk, stride=d), pl.ds(prev, blk, stride=d)
                    lrows = pl.ds(start - base, blk, stride=d)
                q = qf[rows, :].astype(BF16)
                kk = jnp.concatenate([kf[prows, :], kf[rows, :]], axis=0).astype(BF16)
                vv = jnp.concatenate([vf[prows, :], vf[rows, :]], axis=0).astype(BF16)
                s = _dot_nt(q, kk) + bias_ref[g, jnp.where(has_prev, 0, 1)]
                m_t = jnp.max(s, axis=1, keepdims=True)
                e = jnp.exp2(s - m_t)
                l_t = jnp.sum(e, axis=1, keepdims=True)
                m_s[g, lrows, :] = jnp.broadcast_to(m_t, (blk, LANES))
                l_s[g, lrows, :] = jnp.broadcast_to(l_t, (blk, LANES))
                acc_s[g, lrows, :] = _dot(e.astype(BF16), vv)
            return carry

        lax.fori_loop(0, chunk // (blk * unroll), group, 0)

    m = m_s[0]
    for g in range(1, len(dils)):
        m = jnp.maximum(m, m_s[g])
    num = den = None
    for g in range(len(dils)):
        w = jnp.exp2(m_s[g] - m)
        num = acc_s[g] * w if num is None else num + acc_s[g] * w
        den = l_s[g] * w if den is None else den + l_s[g] * w
    o_ref[...] = (num / den).astype(o_ref.dtype)


def dilated_attention(h, bias_tiles, B, S, q_col, k_col, v_col, n_heads, unroll=16):
    blk = Q_BLOCK
    dils = tuple(d for _, d in DILATED_CONFIGS)
    assert all(w // d == blk for w, d in DILATED_CONFIGS)
    chunk = blk * max(dils)
    assert S % chunk == 0 and (chunk // blk) % unroll == 0
    dh = HEAD_DIM_DIL
    ng = len(dils)
    return pl.pallas_call(
        functools.partial(_dil_kernel, blk=blk, chunk=chunk, dils=dils, unroll=unroll),
        grid=(B, n_heads, S // chunk),
        in_specs=[pl.BlockSpec((None, S, dh), lambda b, hh, c: (b, 0, q_col + hh)),
                  pl.BlockSpec((None, S, dh), lambda b, hh, c: (b, 0, k_col + hh)),
                  pl.BlockSpec((None, S, dh), lambda b, hh, c: (b, 0, v_col + hh)),
                  pl.BlockSpec((None, ng, 2, blk, 2 * blk), lambda b, hh, c: (hh, 0, 0, 0, 0))],
        out_specs=pl.BlockSpec((None, chunk, dh), lambda b, hh, c: (b, c, hh)),
        out_shape=jax.ShapeDtypeStruct((B, S, n_heads * dh), BF16),
        scratch_shapes=[pltpu.VMEM((S, dh), F32), pltpu.VMEM((S, dh), F32), pltpu.VMEM((S, dh), F32),
                        pltpu.VMEM((ng, chunk, LANES), F32), pltpu.VMEM((ng, chunk, LANES), F32),
                        pltpu.VMEM((ng, chunk, dh), F32)],
        compiler_params=_cparams(("parallel", "parallel", "arbitrary")),
        name="dilated_attention",
    )(h, h, h, bias_tiles)


def _rel_bucket(dist):
    max_exact = N_BUCKETS // 2
    d_f = jnp.maximum(dist, 1).astype(F32)
    large = max_exact + (jnp.log(d_f / max_exact) / math.log(BUCKET_MAX_DIST / max_exact)
                         * (N_BUCKETS - max_exact)).astype(I32)
    large = jnp.minimum(large, N_BUCKETS - 1)
    return jnp.where(dist < max_exact, dist, large)


def _bias_by_distance(rel_bias, n):
    return rel_bias.astype(F32)[_rel_bucket(jnp.arange(n, dtype=I32))].T


def _toeplitz_band(g, nblk):
    blk = Q_BLOCK
    period = blk * (nblk + 2)
    m = jnp.arange(period, dtype=I32)
    m = jnp.where(m >= period - blk, m - period, m)
    v = g[:, jnp.clip(blk * nblk - m, 0, g.shape[1] - 1)]
    flat = jnp.tile(v, (1, blk + 1))[:, :blk * (period - 1)]
    return flat.reshape(g.shape[0], blk, period - 1)[:, :, :blk * (nblk + 1)]


def _causal_bias_vectors(rel_bias, nq):
    blk = Q_BLOCK
    g = _bias_by_distance(rel_bias, nq * blk) * LOG2E
    m = np.arange(2 * blk)
    m = np.where(m >= blk, m - 2 * blk, m)
    dist = blk * (nq - np.arange(nq + 1))[:, None] - m[None, :]
    return g[:, np.clip(dist, 0, nq * blk - 1)]


def _dilated_bias_tiles(rel_bias):
    blk = Q_BLOCK
    g = _bias_by_distance(rel_bias, 2 * blk * max(d for _, d in DILATED_CONFIGS)) * LOG2E
    j = blk + np.arange(blk)[:, None] - np.arange(2 * blk)[None, :]
    band = (j >= 0) & (j <= blk)
    masks = np.stack([band, band & (np.arange(2 * blk)[None, :] >= blk)])
    tiles = jnp.stack([_toeplitz_band(g[:, ::d][:, :2 * blk], 1) for _, d in DILATED_CONFIGS], axis=1)
    return jnp.where(masks[None, None], tiles[:, :, None], NEG)


def _lane_slabs(w, width):
    K, N = w.shape
    w = w.reshape(K, N // width, width)
    return jnp.pad(w, ((0, 0), (0, 0), (0, LANES - width))).reshape(K, N // width * LANES)


def even_mixer(x, w_in, w_out, rel_bias, B, S):
    half = N_HEADS_SB * HEAD_DIM_SB
    n_main = 6 * half
    n_qi = N_HEADS_IDX * HEAD_DIM_IDX
    slab = half // LANES
    col_scale = _col_scale(n_main, [(0, half, HEAD_DIM_SB ** -0.5 * LOG2E),
                                    (3 * half, 4 * half, HEAD_DIM_DSA ** -0.5 * LOG2E)])
    h = matmul(x, w_in[:, :n_main].astype(BF16), col_scale, BF16).reshape(B, S, n_main)
    w_idx = w_in[:, n_main:]
    w_idx = jnp.concatenate([_lane_slabs(w_idx[:, :n_qi], HEAD_DIM_IDX),
                             _lane_slabs(w_idx[:, n_qi:n_qi + HEAD_DIM_IDX], HEAD_DIM_IDX),
                             _lane_slabs(w_idx[:, n_qi + HEAD_DIM_IDX:], N_HEADS_IDX)], axis=1)
    idx_scale = _col_scale(w_idx.shape[1], [(0, N_HEADS_IDX * LANES, HEAD_DIM_IDX ** -0.5)])
    hi = matmul_slabs(x, w_idx.astype(BF16), idx_scale, B, S)
    madd = dsa_select(hi, min(TOPK_MAX, S // 4))
    oa = stick_breaking(h, B, S, 0, slab, 2 * slab, N_HEADS_SB)
    ob = dsa_attention(h, madd, _causal_bias_vectors(rel_bias, S // Q_BLOCK), B, S,
                       3 * slab, 4 * slab, 5 * slab)
    w_out = w_out.astype(BF16)
    return [oa.reshape(B * S, half), ob.reshape(B * S, half)], [w_out[:half], w_out[half:]]


def odd_mixer(xb, w_in, w_out, rel_bias, B, S):
    width = N_HEADS_DIL * HEAD_DIM_DIL
    col_scale = _col_scale(3 * width, [(0, width, HEAD_DIM_DIL ** -0.5 * LOG2E)])
    h = matmul(xb, w_in.astype(BF16), col_scale, BF16).reshape(B, S, 3 * width)
    o = dilated_attention(h, _dilated_bias_tiles(rel_bias), B, S, 0, N_HEADS_DIL, 2 * N_HEADS_DIL,
                          N_HEADS_DIL)
    return [o.reshape(B * S, width)], [w_out.astype(BF16)]


def kernel(x, even_w_in, even_w_out, odd_w_in, odd_w_out, rel_bias, ln_mix_g, ln_mix_b,
           ffn_w1, ffn_w2, ln_ffn_g, ln_ffn_b):
    B, S, D = x.shape
    x2 = x.reshape(B * S, D)
    xb = x2
    for layer in range(DEPTH):
        if layer % 2 == 0:
            a_list, w_list = even_mixer(xb, even_w_in[layer // 2], even_w_out[layer // 2],
                                        rel_bias, B, S)
        else:
            a_list, w_list = odd_mixer(xb, odd_w_in[layer // 2], odd_w_out[layer // 2],
                                       rel_bias, B, S)
        x2, xb = proj_residual_ln(a_list, w_list, x2, ln_mix_g[layer], ln_mix_b[layer])
        x2, xb = ffn_residual_ln(x2, xb, ffn_w1[layer].astype(BF16), ffn_w2[layer].astype(BF16),
                                 ln_ffn_g[layer], ln_ffn_b[layer])
    return x2.reshape(B, S, D)
```

```python
import functools
import math

import jax
import jax.numpy as jnp
import numpy as np
from jax import lax
from jax.experimental import pallas as pl
from jax.experimental.pallas import tpu as pltpu

F32 = jnp.float32
BF16 = jnp.bfloat16
I32 = jnp.int32

Q_BLOCK = 128
HEAD_DIM_SB = 128
N_HEADS_SB = 8
N_HEADS_DSA = 16
HEAD_DIM_DSA = 64
N_HEADS_IDX = 8
HEAD_DIM_IDX = 64
TOPK_MAX = 256
N_HEADS_DIL = 16
HEAD_DIM_DIL = 128
DILATED_CONFIGS = ((128, 1), (512, 4), (2048, 16))
N_BUCKETS = 32
BUCKET_MAX_DIST = 2048
DEPTH = 2
DN_ALPHA = (2 * DEPTH) ** 0.25
LN_EPS = 1e-5
NEG = -1e30
INT_MIN = -(2 ** 31)
LOG2E = math.log2(math.e)

LANES = 128
VMEM_LIMIT_BYTES = 52 * 1024 * 1024


def _cparams(sem):
    return pltpu.CompilerParams(dimension_semantics=sem, vmem_limit_bytes=VMEM_LIMIT_BYTES)


def _dot(a, b):
    return jnp.dot(a, b, preferred_element_type=F32)


def _dot_nt(a, b):
    return lax.dot_general(a, b, (((1,), (1,)), ((), ())), preferred_element_type=F32)


def _mm_kernel(a_ref, w_ref, s_ref, o_ref, wb_ref):
    @pl.when(pl.program_id(1) == 0)
    def _():
        wb_ref[...] = w_ref[...].astype(BF16)

    acc = _dot(a_ref[...].astype(BF16), wb_ref[...])
    o_ref[...] = (acc * s_ref[...]).astype(o_ref.dtype)


def matmul(a, w, layer, col_scale, out_dtype, tm=1024, tn=1024):
    M, K = a.shape
    N = col_scale.shape[0]
    tm, tn = min(tm, M), min(tn, N)
    assert M % tm == 0 and N % tn == 0 and N <= w.shape[2]
    return pl.pallas_call(
        _mm_kernel,
        grid=(N // tn, M // tm),
        in_specs=[pl.BlockSpec((tm, K), lambda j, i: (i, 0)),
                  pl.BlockSpec((None, K, tn), lambda j, i: (layer, 0, j)),
                  pl.BlockSpec((1, tn), lambda j, i: (0, j))],
        out_specs=pl.BlockSpec((tm, tn), lambda j, i: (i, j)),
        out_shape=jax.ShapeDtypeStruct((M, N), out_dtype),
        scratch_shapes=[pltpu.VMEM((K, tn), BF16)],
        compiler_params=_cparams(("parallel", "arbitrary")),
        name="proj_matmul",
    )(a, w, col_scale.reshape(1, N))


def _col_scale(n, scaled):
    s = np.ones((n,), np.float32)
    for lo, hi, f in scaled:
        s[lo:hi] = f
    return jnp.asarray(s)


def _mm_slab_kernel(a_ref, b_ref, s_ref, o_ref):
    acc = _dot(a_ref[...].astype(BF16), b_ref[...]) * s_ref[...]
    for j in range(o_ref.shape[0]):
        o_ref[j] = acc[:, j * LANES:(j + 1) * LANES]


def matmul_slabs(a, b, col_scale, B, S, tm=1024, tn=1280):
    M, K = a.shape
    N = b.shape[1]
    tm, tn = min(tm, S), min(tn, N)
    assert M == B * S and S % tm == 0 and N % tn == 0 and tn % LANES == 0
    steps = S // tm
    return pl.pallas_call(
        _mm_slab_kernel,
        grid=(B, steps, N // tn),
        in_specs=[pl.BlockSpec((tm, K), lambda b, i, j: (b * steps + i, 0)),
                  pl.BlockSpec((K, tn), lambda b, i, j: (0, j)),
                  pl.BlockSpec((1, tn), lambda b, i, j: (0, j))],
        out_specs=pl.BlockSpec((None, tn // LANES, tm, LANES), lambda b, i, j: (b, j, i, 0)),
        out_shape=jax.ShapeDtypeStruct((B, N // LANES, S, LANES), F32),
        compiler_params=_cparams(("parallel", "parallel", "arbitrary")),
        name="indexer_proj",
    )(a, b, col_scale.reshape(1, N))


def _layer_norm(y, g, b):
    mu = jnp.mean(y, axis=-1, keepdims=True)
    yc = y - mu
    var = jnp.mean(jnp.square(yc), axis=-1, keepdims=True)
    return yc * lax.rsqrt(var + LN_EPS) * g + b


def _proj_ln_kernel(*refs, n_in):
    a_refs, w_refs = refs[:n_in], refs[n_in:2 * n_in]
    x_ref, g_ref, b_ref, y_ref, yb_ref = refs[2 * n_in:]
    acc = _dot(a_refs[0][...], w_refs[0][...])
    for a_ref, w_ref in zip(a_refs[1:], w_refs[1:]):
        acc = acc + _dot(a_ref[...], w_ref[...])
    y = _layer_norm(DN_ALPHA * x_ref[...] + acc, g_ref[...], b_ref[...])
    y_ref[...] = y
    yb_ref[...] = y.astype(BF16)


def proj_residual_ln(a_list, w, layer, x, g, b, tm=512):
    M, D = x.shape
    tm = min(tm, M)
    assert M % tm == 0
    n_in = len(a_list)
    kw = a_list[0].shape[1]
    assert all(a.shape[1] == kw for a in a_list) and n_in * kw == w.shape[1]
    in_specs = ([pl.BlockSpec((tm, kw), lambda i: (i, 0)) for _ in a_list]
                + [pl.BlockSpec((None, kw, D), lambda i, r=r: (layer, r, 0)) for r in range(n_in)]
                + [pl.BlockSpec((tm, D), lambda i: (i, 0)),
                   pl.BlockSpec((1, D), lambda i: (0, 0)),
                   pl.BlockSpec((1, D), lambda i: (0, 0))])
    return pl.pallas_call(
        functools.partial(_proj_ln_kernel, n_in=n_in),
        grid=(M // tm,),
        in_specs=in_specs,
        out_specs=[pl.BlockSpec((tm, D), lambda i: (i, 0)),
                   pl.BlockSpec((tm, D), lambda i: (i, 0))],
        out_shape=[jax.ShapeDtypeStruct((M, D), F32), jax.ShapeDtypeStruct((M, D), BF16)],
        compiler_params=_cparams(("parallel",)),
        name="out_proj_ln",
    )(*a_list, *([w] * n_in), x, g.reshape(1, D), b.reshape(1, D))


def _ffn_kernel(xb_ref, x_ref, w1_ref, w2_ref, g_ref, b_ref, y_ref, yb_ref, acc_ref):
    f = pl.program_id(1)

    @pl.when(f == 0)
    def _():
        acc_ref[...] = jnp.zeros_like(acc_ref)

    h = _dot(xb_ref[...], w1_ref[...])
    h = jnp.square(jnp.maximum(h, 0.0)).astype(BF16)
    acc_ref[...] += _dot(h, w2_ref[...])

    @pl.when(f == pl.num_programs(1) - 1)
    def _():
        y = _layer_norm(DN_ALPHA * x_ref[...] + acc_ref[...], g_ref[...], b_ref[...])
        y_ref[...] = y
        yb_ref[...] = y.astype(BF16)


def ffn_residual_ln(x, xb, w1, w2, layer, g, b, tm=512, tf=1024):
    M, D = x.shape
    F = w1.shape[2]
    tm, tf = min(tm, M), min(tf, F)
    assert M % tm == 0 and F % tf == 0
    return pl.pallas_call(
        _ffn_kernel,
        grid=(M // tm, F // tf),
        in_specs=[pl.BlockSpec((tm, D), lambda i, f: (i, 0)),
                  pl.BlockSpec((tm, D), lambda i, f: (i, 0)),
                  pl.BlockSpec((None, D, tf), lambda i, f: (layer, 0, f)),
                  pl.BlockSpec((None, tf, D), lambda i, f: (layer, f, 0)),
                  pl.BlockSpec((1, D), lambda i, f: (0, 0)),
                  pl.BlockSpec((1, D), lambda i, f: (0, 0))],
        out_specs=[pl.BlockSpec((tm, D), lambda i, f: (i, 0)),
                   pl.BlockSpec((tm, D), lambda i, f: (i, 0))],
        out_shape=[jax.ShapeDtypeStruct((M, D), F32), jax.ShapeDtypeStruct((M, D), BF16)],
        scratch_shapes=[pltpu.VMEM((tm, D), F32)],
        compiler_params=_cparams(("parallel", "arbitrary")),
        name="ffn_ln",
    )(xb, x, w1, w2, g.reshape(1, D), b.reshape(1, D))


def _sb_kernel(q_ref, k_ref, v_ref, o_ref, *, tq, tk, sub):
    i = pl.program_id(2)
    q = q_ref[...]
    r = lax.broadcasted_iota(I32, (sub, sub), 0)
    c = lax.broadcasted_iota(I32, (sub, sub), 1)
    tri = jnp.where(r > c, 1.0, 0.0).astype(BF16)
    qpos = i * tq + lax.broadcasted_iota(I32, (tq, tk), 0)
    koff = lax.broadcasted_iota(I32, (tq, tk), 1)

    def chunk(kc, carry, acc, diag):
        start = pl.multiple_of(kc * tk, tk)
        k = k_ref[pl.ds(start, tk), :]
        v = v_ref[pl.ds(start, tk), :]
        z = _dot_nt(q, k)
        softplus = jnp.log(1.0 + jnp.exp2(-jnp.abs(z))) * LOG2E
        log_beta = jnp.minimum(z, 0.0) - softplus
        log_keep = log_beta - z
        if diag:
            strict = (start + koff) < qpos
            log_keep = jnp.where(strict, log_keep, 0.0)
        laters = []
        for s in reversed(range(tk // sub)):
            lk = log_keep[:, s * sub:(s + 1) * sub]
            later = _dot(lk.astype(BF16), tri) + carry
            laters.append(later)
            carry = later[:, 0:1] + lk[:, 0:1]
        later = jnp.concatenate(laters[::-1], axis=1)
        a = jnp.exp2(log_beta + later)
        if diag:
            a = jnp.where(strict, a, 0.0)
        acc = acc + _dot(a.astype(BF16), v)
        return carry, acc

    kd = (i * tq + tq - 1) // tk
    carry = jnp.zeros((tq, 1), F32)
    acc = jnp.zeros((tq, q.shape[1]), F32)
    carry, acc = chunk(kd, carry, acc, True)
    carry, acc = lax.fori_loop(0, kd, lambda it, st: chunk(kd - 1 - it, st[0], st[1], False),
                               (carry, acc))
    o_ref[...] = acc.astype(o_ref.dtype)


def stick_breaking(h, B, S, q_col, k_col, v_col, n_heads, tq=1024, tk=1024, sub=256):
    dh = HEAD_DIM_SB
    tq, tk = min(tq, S), min(tk, S)
    sub = min(sub, tk)
    assert S % tq == 0 and S % tk == 0 and tk % sub == 0 and tk % tq == 0
    return pl.pallas_call(
        functools.partial(_sb_kernel, tq=tq, tk=tk, sub=sub),
        grid=(B, n_heads, S // tq),
        in_specs=[pl.BlockSpec((None, tq, dh), lambda b, hh, i: (b, i, q_col + hh)),
                  pl.BlockSpec((None, S, dh), lambda b, hh, i: (b, 0, k_col + hh)),
                  pl.BlockSpec((None, S, dh), lambda b, hh, i: (b, 0, v_col + hh))],
        out_specs=pl.BlockSpec((None, tq, dh), lambda b, hh, i: (b, i, hh)),
        out_shape=jax.ShapeDtypeStruct((B, S, n_heads * dh), BF16),
        compiler_params=_cparams(("parallel", "parallel", "arbitrary")),
        name="stick_breaking",
    )(h, h, h)


def _sel_kernel(qi_ref, ki_ref, wi_ref, o_ref, keys_ref, *, blk, nk, ch, topk, w_scale):
    n = pl.program_id(1)
    nh = qi_ref.shape[0]
    cw = ch * blk
    last = n // ch
    qall = qi_ref[...].reshape(nh * blk, LANES).astype(BF16)
    w8 = wi_ref[...].T[:nh, :] * w_scale
    row = lax.broadcasted_iota(I32, (blk, blk), 0)
    col = lax.broadcasted_iota(I32, (blk, blk), 1)
    tri = jnp.where(col < row, 1.0, 0.0).astype(BF16)
    qpos = n * blk + lax.broadcasted_iota(I32, (cw, blk), 1)
    koff = lax.broadcasted_iota(I32, (cw, blk), 0)

    def score_keys(cc, diag):
        start = pl.multiple_of(cc * cw, cw)
        d = _dot_nt(ki_ref[pl.ds(start, cw), :].astype(BF16), qall)
        d = jnp.maximum(d, 0.0)
        sc = d[:, 0:blk] * w8[0:1, :]
        for hh in range(1, nh):
            sc = sc + d[:, hh * blk:(hh + 1) * blk] * w8[hh:hh + 1, :]
        sc = jnp.where(sc == 0.0, 0.0, sc)
        bits = pltpu.bitcast(sc, I32)
        key = jnp.where(bits < 0, bits ^ 0x7FFFFFFF, bits)
        if diag:
            key = jnp.where(start + koff <= qpos, key, INT_MIN)
        keys_ref[pl.ds(cc * ch, ch)] = key.reshape(ch, blk, blk)

    score_keys(last, True)
    lax.fori_loop(0, last, lambda cc, c: (score_keys(cc, False), c)[1], 0)

    def count(pred):
        def body(cc, c):
            kk = keys_ref[pl.ds(cc * ch, ch)]
            for j in range(ch):
                c = c + jnp.where(pred(kk[j]), 1, 0)
            return c
        c = lax.fori_loop(0, last + 1, body, jnp.zeros((blk, blk), I32))
        return jnp.sum(c, axis=0, keepdims=True)

    c0 = count(lambda key: key >= 0)
    theta = jnp.where(c0 >= topk, 0, INT_MIN).astype(I32)

    def bit_step(i, theta):
        cand = theta + lax.shift_left(jnp.int32(1), 30 - i)
        c = count(lambda key: key >= cand)
        return jnp.where(c >= topk, cand, theta)

    theta = lax.fori_loop(0, 31, bit_step, theta)
    need = (topk - count(lambda key: key > theta)).astype(F32)

    def emit(cc, carry, diag):
        kk = keys_ref[pl.ds(cc * ch, ch)]
        for j in range(ch):
            key = kk[j]
            eqf = jnp.where(key == theta, 1.0, 0.0)
            rank = _dot(tri, eqf.astype(BF16)) + carry
            sel = jnp.where(key > theta, 1.0, jnp.where(rank < need, eqf, 0.0))
            if diag:
                kb = cc * ch + j
                sel = jnp.where(kb * blk + row <= n * blk + col, sel, 0.0)
            madd = jnp.where(sel > 0.5, 0.0, -jnp.inf)
            o_ref[cc * ch + j] = madd.T.astype(o_ref.dtype)
            carry = carry + jnp.sum(eqf, axis=0, keepdims=True)
        return carry

    carry = lax.fori_loop(0, last, lambda cc, c: emit(cc, c, False), jnp.zeros((1, blk), F32))
    emit(last, carry, True)

    def fill(kb, c):
        o_ref[kb] = jnp.full((blk, blk), -jnp.inf, o_ref.dtype)
        return c

    lax.fori_loop((last + 1) * ch, nk, fill, 0)


def dsa_select(hi, topk, ch=4):
    B, nslab, S, _ = hi.shape
    nh = N_HEADS_IDX
    assert nslab == nh + 2
    blk = Q_BLOCK
    nq = S // blk
    ch = min(ch, nq)
    assert nq % ch == 0
    return pl.pallas_call(
        functools.partial(_sel_kernel, blk=blk, nk=nq, ch=ch, topk=topk, w_scale=N_HEADS_IDX ** -0.5),
        grid=(B, nq),
        in_specs=[pl.BlockSpec((None, nh, blk, LANES), lambda b, i: (b, 0, i, 0)),
                  pl.BlockSpec((None, None, S, LANES), lambda b, i: (b, nh, 0, 0)),
                  pl.BlockSpec((None, None, blk, LANES), lambda b, i: (b, nh + 1, i, 0))],
        out_specs=pl.BlockSpec((None, None, nq, blk, blk), lambda b, i: (b, i, 0, 0, 0)),
        out_shape=jax.ShapeDtypeStruct((B, nq, nq, blk, blk), BF16),
        scratch_shapes=[pltpu.VMEM((nq, blk, blk), I32)],
        compiler_params=_cparams(("parallel", "arbitrary")),
        name="dsa_select",
    )(hi, hi, hi)


def _dsa_kernel(q_ref, k_ref, v_ref, m_ref, vec_ref, o_ref, bias_ref, *, tq, tk, dh):
    i = pl.program_id(2)
    blk = m_ref.shape[-1]
    rq, rk = tq // blk, tk // blk
    nb = bias_ref.shape[1] - 1

    @pl.when((i == 0) & (pl.program_id(1) == 0))
    def _():
        def expand(kk, c):
            for hh in range(2):
                rows = jnp.broadcast_to(vec_ref[hh, pl.ds(kk, 1), :], (blk, 2 * blk))
                bias_ref[hh, kk] = pltpu.roll(rows, 0, 1, stride=1, stride_axis=0)[:, :blk]
            return c
        lax.fori_loop(0, nb + 1, expand, 0)

    q = q_ref[...]
    lane = lax.broadcasted_iota(I32, q.shape, 1)
    halves = (lane < dh, lane >= dh)
    qs = [jnp.where(hm, q, jnp.zeros_like(q)) for hm in halves]
    vlane = lax.broadcasted_iota(I32, (tk, LANES), 1)
    vhalves = (vlane < dh, vlane >= dh)

    def tiles(load):
        return jnp.concatenate(
            [jnp.concatenate([load(r, c) for c in range(rk)], axis=1) for r in range(rq)], axis=0)

    def body(kc, carry):
        start = pl.multiple_of(kc * tk, tk)
        k = k_ref[pl.ds(start, tk), :]
        v = v_ref[pl.ds(start, tk), :]
        madd = tiles(lambda r, c: m_ref[r, kc * rk + c]).astype(F32)
        out = []
        for hh in range(2):
            m, acc = carry[hh]
            bias = tiles(lambda r, c: bias_ref[hh, jnp.minimum(nb - (i * rq + r) + (kc * rk + c), nb)])
            s = _dot_nt(qs[hh], k) + bias + madd
            m_new = jnp.maximum(m, jnp.max(s, axis=1, keepdims=True))
            p = jnp.exp2(s - m_new)
            vh = jnp.where(vhalves[hh], v, jnp.ones_like(v))
            acc = acc * jnp.exp2(m - m_new) + _dot(p.astype(BF16), vh)
            out.append((m_new, acc))
        return tuple(out)

    init = tuple((jnp.full((tq, 1), NEG, F32), jnp.zeros((tq, LANES), F32)) for _ in range(2))
    n_chunks = (i * tq + tq - 1) // tk + 1
    (_, a0), (_, a1) = lax.fori_loop(0, n_chunks, body, init)
    o_ref[...] = jnp.where(halves[0], a0 / a0[:, dh:dh + 1], a1 / a1[:, 0:1]).astype(o_ref.dtype)


def dsa_attention(h, madd, bias_vecs, B, S, q_col, k_col, v_col, tq=1024, tk=1024):
    blk = Q_BLOCK
    nq = S // blk
    tq, tk = min(tq, S), min(tk, S)
    assert S % tq == 0 and S % tk == 0 and tq % blk == 0 and tk % blk == 0
    n_pairs = N_HEADS_DSA * HEAD_DIM_DSA // LANES
    assert bias_vecs.shape == (2 * n_pairs, nq + 1, 2 * blk)
    return pl.pallas_call(
        functools.partial(_dsa_kernel, tq=tq, tk=tk, dh=HEAD_DIM_DSA),
        grid=(n_pairs, B, S // tq),
        in_specs=[pl.BlockSpec((None, tq, LANES), lambda p, b, i: (b, i, q_col + p)),
                  pl.BlockSpec((None, S, LANES), lambda p, b, i: (b, 0, k_col + p)),
                  pl.BlockSpec((None, S, LANES), lambda p, b, i: (b, 0, v_col + p)),
                  pl.BlockSpec((None, tq // blk, nq, blk, blk), lambda p, b, i: (b, i, 0, 0, 0)),
                  pl.BlockSpec((2, nq + 1, 2 * blk), lambda p, b, i: (p, 0, 0))],
        out_specs=pl.BlockSpec((None, tq, LANES), lambda p, b, i: (b, i, p)),
        out_shape=jax.ShapeDtypeStruct((B, S, n_pairs * LANES), BF16),
        scratch_shapes=[pltpu.VMEM((2, nq + 1, blk, blk), F32)],
        compiler_params=_cparams(("parallel", "arbitrary", "arbitrary")),
        name="dsa_attention",
    )(h, h, h, madd, bias_vecs)


def _dil_kernel(q_ref, k_ref, v_ref, bias_ref, o_ref, qf, kf, vf, m_s, l_s, acc_s,
                *, blk, chunk, dils, unroll):
    c = pl.program_id(2)
    base = pl.multiple_of(c * chunk, chunk)

    @pl.when(c == 0)
    def _():
        qf[...] = q_ref[...].astype(F32)
        kf[...] = k_ref[...].astype(F32)
        vf[...] = v_ref[...].astype(F32)

    for g, d in enumerate(dils):
        span = blk * d

        def group(it, carry, g=g, d=d, span=span):
            for jj in range(unroll):
                t = it * unroll + jj
                u = t // d
                r = t - u * d
                start = base + u * span + r
                has_prev = start >= span
                prev = jnp.where(has_prev, start - span, start)
                if d == 1:
                    start = pl.multiple_of(start, blk)
                    prev = pl.multiple_of(prev, blk)
                    rows, prows = pl.ds(start, blk), pl.ds(prev, blk)
                    lrows = pl.ds(pl.multiple_of(start - base, blk), blk)
                else:
                    rows, prows = pl.ds(start, blk, stride=d), pl.ds(prev, blk, stride=d)
                    lrows = pl.ds(start - base, blk, stride=d)
                q = qf[rows, :].astype(BF16)
                kk = jnp.concatenate([kf[prows, :], kf[rows, :]], axis=0).astype(BF16)
                vv = jnp.concatenate([vf[prows, :], vf[rows, :]], axis=0).astype(BF16)
                s = _dot_nt(q, kk) + bias_ref[g, jnp.where(has_prev, 0, 1)]
                m_t = jnp.max(s, axis=1, keepdims=True)
                e = jnp.exp2(s - m_t)
                l_t = jnp.sum(e, axis=1, keepdims=True)
                m_s[g, lrows, :] = jnp.broadcast_to(m_t, (blk, LANES))
                l_s[g, lrows, :] = jnp.broadcast_to(l_t, (blk, LANES))
                acc_s[g, lrows, :] = _dot(e.astype(BF16), vv)
            return carry

        lax.fori_loop(0, chunk // (blk * unroll), group, 0)

    m = m_s[0]
    for g in range(1, len(dils)):
        m = jnp.maximum(m, m_s[g])
    num = den = None
    for g in range(len(dils)):
        w = jnp.exp2(m_s[g] - m)
        num = acc_s[g] * w if num is None else num + acc_s[g] * w
        den = l_s[g] * w if den is None else den + l_s[g] * w
    o_ref[...] = (num / den).astype(o_ref.dtype)


def dilated_attention(h, bias_tiles, B, S, q_col, k_col, v_col, n_heads, unroll=16):
    blk = Q_BLOCK
    dils = tuple(d for _, d in DILATED_CONFIGS)
    assert all(w // d == blk for w, d in DILATED_CONFIGS)
    chunk = blk * max(dils)
    assert S % chunk == 0 and (chunk // blk) % unroll == 0
    dh = HEAD_DIM_DIL
    ng = len(dils)
    return pl.pallas_call(
        functools.partial(_dil_kernel, blk=blk, chunk=chunk, dils=dils, unroll=unroll),
        grid=(B, n_heads, S // chunk),
        in_specs=[pl.BlockSpec((None, S, dh), lambda b, hh, c: (b, 0, q_col + hh)),
                  pl.BlockSpec((None, S, dh), lambda b, hh, c: (b, 0, k_col + hh)),
                  pl.BlockSpec((None, S, dh), lambda b, hh, c: (b, 0, v_col + hh)),
                  pl.BlockSpec((None, ng, 2, blk, 2 * blk), lambda b, hh, c: (hh, 0, 0, 0, 0))],
        out_specs=pl.BlockSpec((None, chunk, dh), lambda b, hh, c: (b, c, hh)),
        out_shape=jax.ShapeDtypeStruct((B, S, n_heads * dh), BF16),
        scratch_shapes=[pltpu.VMEM((S, dh), F32), pltpu.VMEM((S, dh), F32), pltpu.VMEM((S, dh), F32),
                        pltpu.VMEM((ng, chunk, LANES), F32), pltpu.VMEM((ng, chunk, LANES), F32),
                        pltpu.VMEM((ng, chunk, dh), F32)],
        compiler_params=_cparams(("parallel", "parallel", "arbitrary")),
        name="dilated_attention",
    )(h, h, h, bias_tiles)


def _rel_bucket(dist):
    max_exact = N_BUCKETS // 2
    d_f = jnp.maximum(dist, 1).astype(F32)
    large = max_exact + (jnp.log(d_f / max_exact) / math.log(BUCKET_MAX_DIST / max_exact)
                         * (N_BUCKETS - max_exact)).astype(I32)
    large = jnp.minimum(large, N_BUCKETS - 1)
    return jnp.where(dist < max_exact, dist, large)


def _bias_by_distance(rel_bias, n):
    return rel_bias.astype(F32)[_rel_bucket(jnp.arange(n, dtype=I32))].T


def _toeplitz_band(g, nblk):
    blk = Q_BLOCK
    period = blk * (nblk + 2)
    m = jnp.arange(period, dtype=I32)
    m = jnp.where(m >= period - blk, m - period, m)
    v = g[:, jnp.clip(blk * nblk - m, 0, g.shape[1] - 1)]
    flat = jnp.tile(v, (1, blk + 1))[:, :blk * (period - 1)]
    return flat.reshape(g.shape[0], blk, period - 1)[:, :, :blk * (nblk + 1)]


def _causal_bias_vectors(rel_bias, nq):
    blk = Q_BLOCK
    g = _bias_by_distance(rel_bias, nq * blk) * LOG2E
    m = np.arange(2 * blk)
    m = np.where(m >= blk, m - 2 * blk, m)
    dist = blk * (nq - np.arange(nq + 1))[:, None] - m[None, :]
    return g[:, np.clip(dist, 0, nq * blk - 1)]


def _dilated_bias_tiles(rel_bias):
    blk = Q_BLOCK
    g = _bias_by_distance(rel_bias, 2 * blk * max(d for _, d in DILATED_CONFIGS)) * LOG2E
    j = blk + np.arange(blk)[:, None] - np.arange(2 * blk)[None, :]
    band = (j >= 0) & (j <= blk)
    masks = np.stack([band, band & (np.arange(2 * blk)[None, :] >= blk)])
    tiles = jnp.stack([_toeplitz_band(g[:, ::d][:, :2 * blk], 1) for _, d in DILATED_CONFIGS], axis=1)
    return jnp.where(masks[None, None], tiles[:, :, None], NEG)


def _lane_slabs(w, width):
    K, N = w.shape
    w = w.reshape(K, N // width, width)
    return jnp.pad(w, ((0, 0), (0, 0), (0, LANES - width))).reshape(K, N // width * LANES)


def even_mixer(x, w_in, idx, rel_bias, B, S):
    half = N_HEADS_SB * HEAD_DIM_SB
    n_main = 6 * half
    n_qi = N_HEADS_IDX * HEAD_DIM_IDX
    slab = half // LANES
    col_scale = _col_scale(n_main, [(0, half, HEAD_DIM_SB ** -0.5 * LOG2E),
                                    (3 * half, 4 * half, HEAD_DIM_DSA ** -0.5 * LOG2E)])
    h = matmul(x, w_in, idx, col_scale, BF16).reshape(B, S, n_main)
    w_idx = w_in[idx, :, n_main:]
    w_idx = jnp.concatenate([_lane_slabs(w_idx[:, :n_qi], HEAD_DIM_IDX),
                             _lane_slabs(w_idx[:, n_qi:n_qi + HEAD_DIM_IDX], HEAD_DIM_IDX),
                             _lane_slabs(w_idx[:, n_qi + HEAD_DIM_IDX:], N_HEADS_IDX)], axis=1)
    idx_scale = _col_scale(w_idx.shape[1], [(0, N_HEADS_IDX * LANES, HEAD_DIM_IDX ** -0.5)])
    hi = matmul_slabs(x, w_idx.astype(BF16), idx_scale, B, S)
    madd = dsa_select(hi, min(TOPK_MAX, S // 4))
    oa = stick_breaking(h, B, S, 0, slab, 2 * slab, N_HEADS_SB)
    ob = dsa_attention(h, madd, _causal_bias_vectors(rel_bias, S // Q_BLOCK), B, S,
                       3 * slab, 4 * slab, 5 * slab)
    return [oa.reshape(B * S, half), ob.reshape(B * S, half)]


def odd_mixer(xb, w_in, idx, rel_bias, B, S):
    width = N_HEADS_DIL * HEAD_DIM_DIL
    col_scale = _col_scale(3 * width, [(0, width, HEAD_DIM_DIL ** -0.5 * LOG2E)])
    h = matmul(xb, w_in, idx, col_scale, BF16).reshape(B, S, 3 * width)
    o = dilated_attention(h, _dilated_bias_tiles(rel_bias), B, S, 0, N_HEADS_DIL, 2 * N_HEADS_DIL,
                          N_HEADS_DIL)
    return [o.reshape(B * S, width)]


def kernel(x, even_w_in, even_w_out, odd_w_in, odd_w_out, rel_bias, ln_mix_g, ln_mix_b,
           ffn_w1, ffn_w2, ln_ffn_g, ln_ffn_b):
    B, S, D = x.shape
    x2 = x.reshape(B * S, D)
    xb = x2
    w_out = (even_w_out.astype(BF16), odd_w_out.astype(BF16))
    w1, w2 = ffn_w1.astype(BF16), ffn_w2.astype(BF16)
    for layer in range(DEPTH):
        idx = layer // 2
        if layer % 2 == 0:
            a_list = even_mixer(xb, even_w_in, idx, rel_bias, B, S)
        else:
            a_list = odd_mixer(xb, odd_w_in, idx, rel_bias, B, S)
        x2, xb = proj_residual_ln(a_list, w_out[layer % 2], idx, x2, ln_mix_g[layer], ln_mix_b[layer])
        x2, xb = ffn_residual_ln(x2, xb, w1, w2, layer, ln_ffn_g[layer], ln_ffn_b[layer])
    return x2.reshape(B, S, D)
```

```python
import functools
import math

import jax
import jax.numpy as jnp
import numpy as np
from jax import lax
from jax.experimental import pallas as pl
from jax.experimental.pallas import tpu as pltpu

F32 = jnp.float32
BF16 = jnp.bfloat16
I32 = jnp.int32

Q_BLOCK = 128
HEAD_DIM_SB = 128
N_HEADS_SB = 8
N_HEADS_DSA = 16
HEAD_DIM_DSA = 64
N_HEADS_IDX = 8
HEAD_DIM_IDX = 64
TOPK_MAX = 256
N_HEADS_DIL = 16
HEAD_DIM_DIL = 128
DILATED_CONFIGS = ((128, 1), (512, 4), (2048, 16))
N_BUCKETS = 32
BUCKET_MAX_DIST = 2048
DEPTH = 2
DN_ALPHA = (2 * DEPTH) ** 0.25
LN_EPS = 1e-5
NEG = -1e30
INT_MIN = -(2 ** 31)
LOG2E = math.log2(math.e)
F32_EXP2_ZERO = -150.0

LANES = 128
VMEM_LIMIT_BYTES = 52 * 1024 * 1024


def _cparams(sem):
    return pltpu.CompilerParams(dimension_semantics=sem, vmem_limit_bytes=VMEM_LIMIT_BYTES)


def _dot(a, b):
    return jnp.dot(a, b, preferred_element_type=F32)


def _dot_nt(a, b):
    return lax.dot_general(a, b, (((1,), (1,)), ((), ())), preferred_element_type=F32)


def _mm_kernel(a_ref, w_ref, s_ref, o_ref, wb_ref):
    @pl.when(pl.program_id(1) == 0)
    def _():
        wb_ref[...] = w_ref[...].astype(BF16)

    acc = _dot(a_ref[...].astype(BF16), wb_ref[...])
    o_ref[...] = (acc * s_ref[...]).astype(o_ref.dtype)


def matmul(a, w, layer, col_scale, out_dtype, tm=1024, tn=1024):
    M, K = a.shape
    N = col_scale.shape[0]
    tm, tn = min(tm, M), min(tn, N)
    assert M % tm == 0 and N % tn == 0 and N <= w.shape[2]
    return pl.pallas_call(
        _mm_kernel,
        grid=(N // tn, M // tm),
        in_specs=[pl.BlockSpec((tm, K), lambda j, i: (i, 0)),
                  pl.BlockSpec((None, K, tn), lambda j, i: (layer, 0, j)),
                  pl.BlockSpec((1, tn), lambda j, i: (0, j))],
        out_specs=pl.BlockSpec((tm, tn), lambda j, i: (i, j)),
        out_shape=jax.ShapeDtypeStruct((M, N), out_dtype),
        scratch_shapes=[pltpu.VMEM((K, tn), BF16)],
        compiler_params=_cparams(("parallel", "arbitrary")),
        name="proj_matmul",
    )(a, w, col_scale.reshape(1, N))


def _col_scale(n, scaled):
    s = np.ones((n,), np.float32)
    for lo, hi, f in scaled:
        s[lo:hi] = f
    return jnp.asarray(s)


def _mm_slab_kernel(a_ref, b_ref, s_ref, o_ref):
    acc = _dot(a_ref[...].astype(BF16), b_ref[...]) * s_ref[...]
    for j in range(o_ref.shape[0]):
        o_ref[j] = acc[:, j * LANES:(j + 1) * LANES]


def matmul_slabs(a, b, col_scale, B, S, tm=1024, tn=1280):
    M, K = a.shape
    N = b.shape[1]
    tm, tn = min(tm, S), min(tn, N)
    assert M == B * S and S % tm == 0 and N % tn == 0 and tn % LANES == 0
    steps = S // tm
    return pl.pallas_call(
        _mm_slab_kernel,
        grid=(B, steps, N // tn),
        in_specs=[pl.BlockSpec((tm, K), lambda b, i, j: (b * steps + i, 0)),
                  pl.BlockSpec((K, tn), lambda b, i, j: (0, j)),
                  pl.BlockSpec((1, tn), lambda b, i, j: (0, j))],
        out_specs=pl.BlockSpec((None, tn // LANES, tm, LANES), lambda b, i, j: (b, j, i, 0)),
        out_shape=jax.ShapeDtypeStruct((B, N // LANES, S, LANES), F32),
        compiler_params=_cparams(("parallel", "parallel", "arbitrary")),
        name="indexer_proj",
    )(a, b, col_scale.reshape(1, N))


def _layer_norm(y, g, b):
    mu = jnp.mean(y, axis=-1, keepdims=True)
    yc = y - mu
    var = jnp.mean(jnp.square(yc), axis=-1, keepdims=True)
    return yc * lax.rsqrt(var + LN_EPS) * g + b


def _proj_ln_kernel(*refs, n_in):
    a_refs, w_refs = refs[:n_in], refs[n_in:2 * n_in]
    x_ref, g_ref, b_ref, y_ref, yb_ref = refs[2 * n_in:]
    acc = _dot(a_refs[0][...], w_refs[0][...])
    for a_ref, w_ref in zip(a_refs[1:], w_refs[1:]):
        acc = acc + _dot(a_ref[...], w_ref[...])
    y = _layer_norm(DN_ALPHA * x_ref[...] + acc, g_ref[...], b_ref[...])
    y_ref[...] = y
    yb_ref[...] = y.astype(BF16)


def proj_residual_ln(a_list, w, layer, x, g, b, tm=512):
    M, D = x.shape
    tm = min(tm, M)
    assert M % tm == 0
    n_in = len(a_list)
    kw = a_list[0].shape[1]
    assert all(a.shape[1] == kw for a in a_list) and n_in * kw == w.shape[1]
    in_specs = ([pl.BlockSpec((tm, kw), lambda i: (i, 0)) for _ in a_list]
                + [pl.BlockSpec((None, kw, D), lambda i, r=r: (layer, r, 0)) for r in range(n_in)]
                + [pl.BlockSpec((tm, D), lambda i: (i, 0)),
                   pl.BlockSpec((1, D), lambda i: (0, 0)),
                   pl.BlockSpec((1, D), lambda i: (0, 0))])
    return pl.pallas_call(
        functools.partial(_proj_ln_kernel, n_in=n_in),
        grid=(M // tm,),
        in_specs=in_specs,
        out_specs=[pl.BlockSpec((tm, D), lambda i: (i, 0)),
                   pl.BlockSpec((tm, D), lambda i: (i, 0))],
        out_shape=[jax.ShapeDtypeStruct((M, D), F32), jax.ShapeDtypeStruct((M, D), BF16)],
        compiler_params=_cparams(("parallel",)),
        name="out_proj_ln",
    )(*a_list, *([w] * n_in), x, g.reshape(1, D), b.reshape(1, D))


def _ffn_kernel(xb_ref, x_ref, w1_ref, w2_ref, g_ref, b_ref, y_ref, yb_ref, acc_ref):
    f = pl.program_id(1)

    @pl.when(f == 0)
    def _():
        acc_ref[...] = jnp.zeros_like(acc_ref)

    h = _dot(xb_ref[...], w1_ref[...])
    h = jnp.square(jnp.maximum(h, 0.0)).astype(BF16)
    acc_ref[...] += _dot(h, w2_ref[...])

    @pl.when(f == pl.num_programs(1) - 1)
    def _():
        y = _layer_norm(DN_ALPHA * x_ref[...] + acc_ref[...], g_ref[...], b_ref[...])
        y_ref[...] = y
        yb_ref[...] = y.astype(BF16)


def ffn_residual_ln(x, xb, w1, w2, layer, g, b, tm=512, tf=1024):
    M, D = x.shape
    F = w1.shape[2]
    tm, tf = min(tm, M), min(tf, F)
    assert M % tm == 0 and F % tf == 0
    return pl.pallas_call(
        _ffn_kernel,
        grid=(M // tm, F // tf),
        in_specs=[pl.BlockSpec((tm, D), lambda i, f: (i, 0)),
                  pl.BlockSpec((tm, D), lambda i, f: (i, 0)),
                  pl.BlockSpec((None, D, tf), lambda i, f: (layer, 0, f)),
                  pl.BlockSpec((None, tf, D), lambda i, f: (layer, f, 0)),
                  pl.BlockSpec((1, D), lambda i, f: (0, 0)),
                  pl.BlockSpec((1, D), lambda i, f: (0, 0))],
        out_specs=[pl.BlockSpec((tm, D), lambda i, f: (i, 0)),
                   pl.BlockSpec((tm, D), lambda i, f: (i, 0))],
        out_shape=[jax.ShapeDtypeStruct((M, D), F32), jax.ShapeDtypeStruct((M, D), BF16)],
        scratch_shapes=[pltpu.VMEM((tm, D), F32)],
        compiler_params=_cparams(("parallel", "arbitrary")),
        name="ffn_ln",
    )(xb, x, w1, w2, g.reshape(1, D), b.reshape(1, D))


def _sb_kernel(q_ref, k_ref, v_ref, o_ref, *, tq, tk, sub):
    i = pl.program_id(2)
    q = q_ref[...]
    r = lax.broadcasted_iota(I32, (sub, sub), 0)
    c = lax.broadcasted_iota(I32, (sub, sub), 1)
    tri = jnp.where(r > c, 1.0, 0.0).astype(BF16)
    qpos = i * tq + lax.broadcasted_iota(I32, (tq, tk), 0)
    koff = lax.broadcasted_iota(I32, (tq, tk), 1)

    def chunk(kc, carry, acc, diag):
        start = pl.multiple_of(kc * tk, tk)
        k = k_ref[pl.ds(start, tk), :]
        v = v_ref[pl.ds(start, tk), :]
        z = _dot_nt(q, k)
        softplus = jnp.log(1.0 + jnp.exp2(-jnp.abs(z))) * LOG2E
        log_beta = jnp.minimum(z, 0.0) - softplus
        log_keep = log_beta - z
        if diag:
            strict = (start + koff) < qpos
            log_keep = jnp.where(strict, log_keep, 0.0)
        laters = []
        for s in reversed(range(tk // sub)):
            lk = log_keep[:, s * sub:(s + 1) * sub]
            later = _dot(lk.astype(BF16), tri) + carry
            laters.append(later)
            carry = later[:, 0:1] + lk[:, 0:1]
        later = jnp.concatenate(laters[::-1], axis=1)
        a = jnp.exp2(log_beta + later)
        if diag:
            a = jnp.where(strict, a, 0.0)
        acc = acc + _dot(a.astype(BF16), v)
        return carry, acc

    kd = (i * tq + tq - 1) // tk
    carry = jnp.zeros((tq, 1), F32)
    acc = jnp.zeros((tq, q.shape[1]), F32)
    carry, acc = chunk(kd, carry, acc, True)

    def live(st):
        return (st[0] < kd) & (jnp.max(st[1]) >= F32_EXP2_ZERO)

    def step(st):
        carry, acc = chunk(kd - 1 - st[0], st[1], st[2], False)
        return st[0] + 1, carry, acc

    _, carry, acc = lax.while_loop(live, step, (jnp.int32(0), carry, acc))
    o_ref[...] = acc.astype(o_ref.dtype)


def stick_breaking(h, B, S, q_col, k_col, v_col, n_heads, tq=512, tk=512, sub=256):
    dh = HEAD_DIM_SB
    tq, tk = min(tq, S), min(tk, S)
    sub = min(sub, tk)
    assert S % tq == 0 and S % tk == 0 and tk % sub == 0 and tk % tq == 0
    return pl.pallas_call(
        functools.partial(_sb_kernel, tq=tq, tk=tk, sub=sub),
        grid=(B, n_heads, S // tq),
        in_specs=[pl.BlockSpec((None, tq, dh), lambda b, hh, i: (b, i, q_col + hh)),
                  pl.BlockSpec((None, S, dh), lambda b, hh, i: (b, 0, k_col + hh)),
                  pl.BlockSpec((None, S, dh), lambda b, hh, i: (b, 0, v_col + hh))],
        out_specs=pl.BlockSpec((None, tq, dh), lambda b, hh, i: (b, i, hh)),
        out_shape=jax.ShapeDtypeStruct((B, S, n_heads * dh), BF16),
        compiler_params=_cparams(("parallel", "parallel", "arbitrary")),
        name="stick_breaking",
    )(h, h, h)


def _sel_kernel(qi_ref, ki_ref, wi_ref, o_ref, keys_ref, *, blk, nk, ch, topk, w_scale):
    n = pl.program_id(1)
    nh = qi_ref.shape[0]
    cw = ch * blk
    last = n // ch
    qall = qi_ref[...].reshape(nh * blk, LANES).astype(BF16)
    w8 = wi_ref[...].T[:nh, :] * w_scale
    row = lax.broadcasted_iota(I32, (blk, blk), 0)
    col = lax.broadcasted_iota(I32, (blk, blk), 1)
    tri = jnp.where(col < row, 1.0, 0.0).astype(BF16)
    qpos = n * blk + lax.broadcasted_iota(I32, (cw, blk), 1)
    koff = lax.broadcasted_iota(I32, (cw, blk), 0)

    def score_keys(cc, diag):
        start = pl.multiple_of(cc * cw, cw)
        d = _dot_nt(ki_ref[pl.ds(start, cw), :].astype(BF16), qall)
        d = jnp.maximum(d, 0.0)
        sc = d[:, 0:blk] * w8[0:1, :]
        for hh in range(1, nh):
            sc = sc + d[:, hh * blk:(hh + 1) * blk] * w8[hh:hh + 1, :]
        sc = jnp.where(sc == 0.0, 0.0, sc)
        bits = pltpu.bitcast(sc, I32)
        key = jnp.where(bits < 0, bits ^ 0x7FFFFFFF, bits)
        if diag:
            key = jnp.where(start + koff <= qpos, key, INT_MIN)
        keys_ref[pl.ds(cc * ch, ch)] = key.reshape(ch, blk, blk)

    score_keys(last, True)
    lax.fori_loop(0, last, lambda cc, c: (score_keys(cc, False), c)[1], 0)

    def count(pred):
        def body(cc, c):
            kk = keys_ref[pl.ds(cc * ch, ch)]
            for j in range(ch):
                c = c + jnp.where(pred(kk[j]), 1, 0)
            return c
        c = lax.fori_loop(0, last + 1, body, jnp.zeros((blk, blk), I32))
        return jnp.sum(c, axis=0, keepdims=True)

    c0 = count(lambda key: key >= 0)
    theta = jnp.where(c0 >= topk, 0, INT_MIN).astype(I32)

    def bit_step(i, theta):
        cand = theta + lax.shift_left(jnp.int32(1), 30 - i)
        c = count(lambda key: key >= cand)
        return jnp.where(c >= topk, cand, theta)

    theta = lax.fori_loop(0, 31, bit_step, theta)
    need = (topk - count(lambda key: key > theta)).astype(F32)

    def emit(cc, carry, diag):
        kk = keys_ref[pl.ds(cc * ch, ch)]
        for j in range(ch):
            key = kk[j]
            eqf = jnp.where(key == theta, 1.0, 0.0)
            rank = _dot(tri, eqf.astype(BF16)) + carry
            sel = jnp.where(key > theta, 1.0, jnp.where(rank < need, eqf, 0.0))
            if diag:
                kb = cc * ch + j
                sel = jnp.where(kb * blk + row <= n * blk + col, sel, 0.0)
            madd = jnp.where(sel > 0.5, 0.0, -jnp.inf)
            o_ref[cc * ch + j] = madd.T.astype(o_ref.dtype)
            carry = carry + jnp.sum(eqf, axis=0, keepdims=True)
        return carry

    carry = lax.fori_loop(0, last, lambda cc, c: emit(cc, c, False), jnp.zeros((1, blk), F32))
    emit(last, carry, True)

    def fill(kb, c):
        o_ref[kb] = jnp.full((blk, blk), -jnp.inf, o_ref.dtype)
        return c

    lax.fori_loop((last + 1) * ch, nk, fill, 0)


def dsa_select(hi, topk, ch=4):
    B, nslab, S, _ = hi.shape
    nh = N_HEADS_IDX
    assert nslab == nh + 2
    blk = Q_BLOCK
    nq = S // blk
    ch = min(ch, nq)
    assert nq % ch == 0
    return pl.pallas_call(
        functools.partial(_sel_kernel, blk=blk, nk=nq, ch=ch, topk=topk, w_scale=N_HEADS_IDX ** -0.5),
        grid=(B, nq),
        in_specs=[pl.BlockSpec((None, nh, blk, LANES), lambda b, i: (b, 0, i, 0)),
                  pl.BlockSpec((None, None, S, LANES), lambda b, i: (b, nh, 0, 0)),
                  pl.BlockSpec((None, None, blk, LANES), lambda b, i: (b, nh + 1, i, 0))],
        out_specs=pl.BlockSpec((None, None, nq, blk, blk), lambda b, i: (b, i, 0, 0, 0)),
        out_shape=jax.ShapeDtypeStruct((B, nq, nq, blk, blk), BF16),
        scratch_shapes=[pltpu.VMEM((nq, blk, blk), I32)],
        compiler_params=_cparams(("parallel", "arbitrary")),
        name="dsa_select",
    )(hi, hi, hi)


def _dsa_kernel(q_ref, k_ref, v_ref, m_ref, vec_ref, o_ref, bias_ref, *, tq, tk, dh):
    i = pl.program_id(2)
    blk = m_ref.shape[-1]
    rq, rk = tq // blk, tk // blk
    nb = bias_ref.shape[1] - 1

    @pl.when((i == 0) & (pl.program_id(1) == 0))
    def _():
        def expand(kk, c):
            for hh in range(2):
                rows = jnp.broadcast_to(vec_ref[hh, pl.ds(kk, 1), :], (blk, 2 * blk))
                bias_ref[hh, kk] = pltpu.roll(rows, 0, 1, stride=1, stride_axis=0)[:, :blk]
            return c
        lax.fori_loop(0, nb + 1, expand, 0)

    q = q_ref[...]
    lane = lax.broadcasted_iota(I32, q.shape, 1)
    halves = (lane < dh, lane >= dh)
    qs = [jnp.where(hm, q, jnp.zeros_like(q)) for hm in halves]
    vlane = lax.broadcasted_iota(I32, (tk, LANES), 1)
    vhalves = (vlane < dh, vlane >= dh)

    def tiles(load):
        return jnp.concatenate(
            [jnp.concatenate([load(r, c) for c in range(rk)], axis=1) for r in range(rq)], axis=0)

    def body(kc, carry):
        start = pl.multiple_of(kc * tk, tk)
        k = k_ref[pl.ds(start, tk), :]
        v = v_ref[pl.ds(start, tk), :]
        madd = tiles(lambda r, c: m_ref[r, kc * rk + c]).astype(F32)
        out = []
        for hh in range(2):
            m, acc = carry[hh]
            bias = tiles(lambda r, c: bias_ref[hh, jnp.minimum(nb - (i * rq + r) + (kc * rk + c), nb)])
            s = _dot_nt(qs[hh], k) + bias + madd
            m_new = jnp.maximum(m, jnp.max(s, axis=1, keepdims=True))
            p = jnp.exp2(s - m_new)
            vh = jnp.where(vhalves[hh], v, jnp.ones_like(v))
            acc = acc * jnp.exp2(m - m_new) + _dot(p.astype(BF16), vh)
            out.append((m_new, acc))
        return tuple(out)

    init = tuple((jnp.full((tq, 1), NEG, F32), jnp.zeros((tq, LANES), F32)) for _ in range(2))
    n_chunks = (i * tq + tq - 1) // tk + 1
    (_, a0), (_, a1) = lax.fori_loop(0, n_chunks, body, init)
    o_ref[...] = jnp.where(halves[0], a0 / a0[:, dh:dh + 1], a1 / a1[:, 0:1]).astype(o_ref.dtype)


def dsa_attention(h, madd, bias_vecs, B, S, q_col, k_col, v_col, tq=1024, tk=1024):
    blk = Q_BLOCK
    nq = S // blk
    tq, tk = min(tq, S), min(tk, S)
    assert S % tq == 0 and S % tk == 0 and tq % blk == 0 and tk % blk == 0
    n_pairs = N_HEADS_DSA * HEAD_DIM_DSA // LANES
    assert bias_vecs.shape == (2 * n_pairs, nq + 1, 2 * blk)
    return pl.pallas_call(
        functools.partial(_dsa_kernel, tq=tq, tk=tk, dh=HEAD_DIM_DSA),
        grid=(n_pairs, B, S // tq),
        in_specs=[pl.BlockSpec((None, tq, LANES), lambda p, b, i: (b, i, q_col + p)),
                  pl.BlockSpec((None, S, LANES), lambda p, b, i: (b, 0, k_col + p)),
                  pl.BlockSpec((None, S, LANES), lambda p, b, i: (b, 0, v_col + p)),
                  pl.BlockSpec((None, tq // blk, nq, blk, blk), lambda p, b, i: (b, i, 0, 0, 0)),
                  pl.BlockSpec((2, nq + 1, 2 * blk), lambda p, b, i: (p, 0, 0))],
        out_specs=pl.BlockSpec((None, tq, LANES), lambda p, b, i: (b, i, p)),
        out_shape=jax.ShapeDtypeStruct((B, S, n_pairs * LANES), BF16),
        scratch_shapes=[pltpu.VMEM((2, nq + 1, blk, blk), F32)],
        compiler_params=_cparams(("parallel", "arbitrary", "arbitrary")),
        name="dsa_attention",
    )(h, h, h, madd, bias_vecs)


def _dil_kernel(q_ref, k_ref, v_ref, bias_ref, o_ref, qf, kf, vf, m_s, l_s, acc_s,
                *, blk, chunk, dils, unroll):
    c = pl.program_id(2)
    base = pl.multiple_of(c * chunk, chunk)

    @pl.when(c == 0)
    def _():
        qf[...] = q_ref[...].astype(F32)
        kf[...] = k_ref[...].astype(F32)
        vf[...] = v_ref[...].astype(F32)

    for g, d in enumerate(dils):
        span = blk * d

        def group(it, carry, g=g, d=d, span=span):
            for jj in range(unroll):
                t = it * unroll + jj
                u = t // d
                r = t - u * d
                start = base + u * span + r
                has_prev = start >= span
                prev = jnp.where(has_prev, start - span, start)
                if d == 1:
                    start = pl.multiple_of(start, blk)
                    prev = pl.multiple_of(prev, blk)
                    rows, prows = pl.ds(start, blk), pl.ds(prev, blk)
                    lrows = pl.ds(pl.multiple_of(start - base, blk), blk)
                else:
                    rows, prows = pl.ds(start, blk, stride=d), pl.ds(prev, blk, stride=d)
                    lrows = pl.ds(start - base, blk, stride=d)
                q = qf[rows, :].astype(BF16)
                kk = jnp.concatenate([kf[prows, :], kf[rows, :]], axis=0).astype(BF16)
                vv = jnp.concatenate([vf[prows, :], vf[rows, :]], axis=0).astype(BF16)
                s = _dot_nt(q, kk) + bias_ref[g, jnp.where(has_prev, 0, 1)]
                m_t = jnp.max(s, axis=1, keepdims=True)
                e = jnp.exp2(s - m_t)
                l_t = jnp.sum(e, axis=1, keepdims=True)
                m_s[g, lrows, :] = jnp.broadcast_to(m_t, (blk, LANES))
                l_s[g, lrows, :] = jnp.broadcast_to(l_t, (blk, LANES))
                acc_s[g, lrows, :] = _dot(e.astype(BF16), vv)
            return carry

        lax.fori_loop(0, chunk // (blk * unroll), group, 0)

    m = m_s[0]
    for g in range(1, len(dils)):
        m = jnp.maximum(m, m_s[g])
    num = den = None
    for g in range(len(dils)):
        w = jnp.exp2(m_s[g] - m)
        num = acc_s[g] * w if num is None else num + acc_s[g] * w
        den = l_s[g] * w if den is None else den + l_s[g] * w
    o_ref[...] = (num / den).astype(o_ref.dtype)


def dilated_attention(h, bias_tiles, B, S, q_col, k_col, v_col, n_heads, unroll=16):
    blk = Q_BLOCK
    dils = tuple(d for _, d in DILATED_CONFIGS)
    assert all(w // d == blk for w, d in DILATED_CONFIGS)
    chunk = blk * max(dils)
    assert S % chunk == 0 and (chunk // blk) % unroll == 0
    dh = HEAD_DIM_DIL
    ng = len(dils)
    return pl.pallas_call(
        functools.partial(_dil_kernel, blk=blk, chunk=chunk, dils=dils, unroll=unroll),
        grid=(B, n_heads, S // chunk),
        in_specs=[pl.BlockSpec((None, S, dh), lambda b, hh, c: (b, 0, q_col + hh)),
                  pl.BlockSpec((None, S, dh), lambda b, hh, c: (b, 0, k_col + hh)),
                  pl.BlockSpec((None, S, dh), lambda b, hh, c: (b, 0, v_col + hh)),
                  pl.BlockSpec((None, ng, 2, blk, 2 * blk), lambda b, hh, c: (hh, 0, 0, 0, 0))],
        out_specs=pl.BlockSpec((None, chunk, dh), lambda b, hh, c: (b, c, hh)),
        out_shape=jax.ShapeDtypeStruct((B, S, n_heads * dh), BF16),
        scratch_shapes=[pltpu.VMEM((S, dh), F32), pltpu.VMEM((S, dh), F32), pltpu.VMEM((S, dh), F32),
                        pltpu.VMEM((ng, chunk, LANES), F32), pltpu.VMEM((ng, chunk, LANES), F32),
                        pltpu.VMEM((ng, chunk, dh), F32)],
        compiler_params=_cparams(("parallel", "parallel", "arbitrary")),
        name="dilated_attention",
    )(h, h, h, bias_tiles)


def _rel_bucket(dist):
    max_exact = N_BUCKETS // 2
    d_f = jnp.maximum(dist, 1).astype(F32)
    large = max_exact + (jnp.log(d_f / max_exact) / math.log(BUCKET_MAX_DIST / max_exact)
                         * (N_BUCKETS - max_exact)).astype(I32)
    large = jnp.minimum(large, N_BUCKETS - 1)
    return jnp.where(dist < max_exact, dist, large)


def _bias_by_distance(rel_bias, n):
    return rel_bias.astype(F32)[_rel_bucket(jnp.arange(n, dtype=I32))].T


def _toeplitz_band(g, nblk):
    blk = Q_BLOCK
    period = blk * (nblk + 2)
    m = jnp.arange(period, dtype=I32)
    m = jnp.where(m >= period - blk, m - period, m)
    v = g[:, jnp.clip(blk * nblk - m, 0, g.shape[1] - 1)]
    flat = jnp.tile(v, (1, blk + 1))[:, :blk * (period - 1)]
    return flat.reshape(g.shape[0], blk, period - 1)[:, :, :blk * (nblk + 1)]


def _causal_bias_vectors(rel_bias, nq):
    blk = Q_BLOCK
    g = _bias_by_distance(rel_bias, nq * blk) * LOG2E
    m = np.arange(2 * blk)
    m = np.where(m >= blk, m - 2 * blk, m)
    dist = blk * (nq - np.arange(nq + 1))[:, None] - m[None, :]
    return g[:, np.clip(dist, 0, nq * blk - 1)]


def _dilated_bias_tiles(rel_bias):
    blk = Q_BLOCK
    g = _bias_by_distance(rel_bias, 2 * blk * max(d for _, d in DILATED_CONFIGS)) * LOG2E
    j = blk + np.arange(blk)[:, None] - np.arange(2 * blk)[None, :]
    band = (j >= 0) & (j <= blk)
    masks = np.stack([band, band & (np.arange(2 * blk)[None, :] >= blk)])
    tiles = jnp.stack([_toeplitz_band(g[:, ::d][:, :2 * blk], 1) for _, d in DILATED_CONFIGS], axis=1)
    return jnp.where(masks[None, None], tiles[:, :, None], NEG)


def _lane_slabs(w, width):
    K, N = w.shape
    w = w.reshape(K, N // width, width)
    return jnp.pad(w, ((0, 0), (0, 0), (0, LANES - width))).reshape(K, N // width * LANES)


def even_mixer(x, w_in, idx, rel_bias, B, S):
    half = N_HEADS_SB * HEAD_DIM_SB
    n_main = 6 * half
    n_qi = N_HEADS_IDX * HEAD_DIM_IDX
    slab = half // LANES
    col_scale = _col_scale(n_main, [(0, half, HEAD_DIM_SB ** -0.5 * LOG2E),
                                    (3 * half, 4 * half, HEAD_DIM_DSA ** -0.5 * LOG2E)])
    h = matmul(x, w_in, idx, col_scale, BF16).reshape(B, S, n_main)
    w_idx = w_in[idx, :, n_main:]
    w_idx = jnp.concatenate([_lane_slabs(w_idx[:, :n_qi], HEAD_DIM_IDX),
                             _lane_slabs(w_idx[:, n_qi:n_qi + HEAD_DIM_IDX], HEAD_DIM_IDX),
                             _lane_slabs(w_idx[:, n_qi + HEAD_DIM_IDX:], N_HEADS_IDX)], axis=1)
    idx_scale = _col_scale(w_idx.shape[1], [(0, N_HEADS_IDX * LANES, HEAD_DIM_IDX ** -0.5)])
    hi = matmul_slabs(x, w_idx.astype(BF16), idx_scale, B, S)
    madd = dsa_select(hi, min(TOPK_MAX, S // 4))
    oa = stick_breaking(h, B, S, 0, slab, 2 * slab, N_HEADS_SB)
    ob = dsa_attention(h, madd, _causal_bias_vectors(rel_bias, S // Q_BLOCK), B, S,
                       3 * slab, 4 * slab, 5 * slab)
    return [oa.reshape(B * S, half), ob.reshape(B * S, half)]


def odd_mixer(xb, w_in, idx, rel_bias, B, S):
    width = N_HEADS_DIL * HEAD_DIM_DIL
    col_scale = _col_scale(3 * width, [(0, width, HEAD_DIM_DIL ** -0.5 * LOG2E)])
    h = matmul(xb, w_in, idx, col_scale, BF16).reshape(B, S, 3 * width)
    o = dilated_attention(h, _dilated_bias_tiles(rel_bias), B, S, 0, N_HEADS_DIL, 2 * N_HEADS_DIL,
                          N_HEADS_DIL)
    return [o.reshape(B * S, width)]


def kernel(x, even_w_in, even_w_out, odd_w_in, odd_w_out, rel_bias, ln_mix_g, ln_mix_b,
           ffn_w1, ffn_w2, ln_ffn_g, ln_ffn_b):
    B, S, D = x.shape
    x2 = x.reshape(B * S, D)
    xb = x2
    w_out = (even_w_out.astype(BF16), odd_w_out.astype(BF16))
    w1, w2 = ffn_w1.astype(BF16), ffn_w2.astype(BF16)
    for layer in range(DEPTH):
        idx = layer // 2
        if layer % 2 == 0:
            a_list = even_mixer(xb, even_w_in, idx, rel_bias, B, S)
        else:
            a_list = odd_mixer(xb, odd_w_in, idx, rel_bias, B, S)
        x2, xb = proj_residual_ln(a_list, w_out[layer % 2], idx, x2, ln_mix_g[layer], ln_mix_b[layer])
        x2, xb = ffn_residual_ln(x2, xb, w1, w2, layer, ln_ffn_g[layer], ln_ffn_b[layer])
    return x2.reshape(B, S, D)
```

```python
import functools
import math

import jax
import jax.numpy as jnp
import numpy as np
from jax import lax
from jax.experimental import pallas as pl
from jax.experimental.pallas import tpu as pltpu

F32 = jnp.float32
BF16 = jnp.bfloat16
I32 = jnp.int32

Q_BLOCK = 128
HEAD_DIM_SB = 128
N_HEADS_SB = 8
N_HEADS_DSA = 16
HEAD_DIM_DSA = 64
N_HEADS_IDX = 8
HEAD_DIM_IDX = 64
TOPK_MAX = 256
N_HEADS_DIL = 16
HEAD_DIM_DIL = 128
DILATED_CONFIGS = ((128, 1), (512, 4), (2048, 16))
N_BUCKETS = 32
BUCKET_MAX_DIST = 2048
DEPTH = 2
DN_ALPHA = (2 * DEPTH) ** 0.25
LN_EPS = 1e-5
NEG = -1e30
INT_MIN = -(2 ** 31)
LOG2E = math.log2(math.e)
F32_EXP2_ZERO = -150.0

LANES = 128
VMEM_LIMIT_BYTES = 52 * 1024 * 1024


def _cparams(sem):
    return pltpu.CompilerParams(dimension_semantics=sem, vmem_limit_bytes=VMEM_LIMIT_BYTES)


def _dot(a, b):
    return jnp.dot(a, b, preferred_element_type=F32)


def _dot_nt(a, b):
    return lax.dot_general(a, b, (((1,), (1,)), ((), ())), preferred_element_type=F32)


def _mm_kernel(a_ref, w_ref, s_ref, o_ref, wb_ref):
    @pl.when(pl.program_id(1) == 0)
    def _():
        wb_ref[...] = w_ref[...].astype(BF16)

    acc = _dot(a_ref[...].astype(BF16), wb_ref[...])
    o_ref[...] = (acc * s_ref[...]).astype(o_ref.dtype)


def matmul(a, w, layer, col_scale, out_dtype, tm=1024, tn=1024):
    M, K = a.shape
    N = col_scale.shape[0]
    tm, tn = min(tm, M), min(tn, N)
    assert M % tm == 0 and N % tn == 0 and N <= w.shape[2]
    return pl.pallas_call(
        _mm_kernel,
        grid=(N // tn, M // tm),
        in_specs=[pl.BlockSpec((tm, K), lambda j, i: (i, 0)),
                  pl.BlockSpec((None, K, tn), lambda j, i: (layer, 0, j)),
                  pl.BlockSpec((1, tn), lambda j, i: (0, j))],
        out_specs=pl.BlockSpec((tm, tn), lambda j, i: (i, j)),
        out_shape=jax.ShapeDtypeStruct((M, N), out_dtype),
        scratch_shapes=[pltpu.VMEM((K, tn), BF16)],
        compiler_params=_cparams(("parallel", "arbitrary")),
        name="proj_matmul",
    )(a, w, col_scale.reshape(1, N))


def _col_scale(n, scaled):
    s = np.ones((n,), np.float32)
    for lo, hi, f in scaled:
        s[lo:hi] = f
    return jnp.asarray(s)


def _mm_slab_kernel(a_ref, b_ref, s_ref, o_ref):
    acc = _dot(a_ref[...].astype(BF16), b_ref[...]) * s_ref[...]
    for j in range(o_ref.shape[0]):
        o_ref[j] = acc[:, j * LANES:(j + 1) * LANES]


def matmul_slabs(a, b, col_scale, B, S, tm=1024, tn=1280):
    M, K = a.shape
    N = b.shape[1]
    tm, tn = min(tm, S), min(tn, N)
    assert M == B * S and S % tm == 0 and N % tn == 0 and tn % LANES == 0
    steps = S // tm
    return pl.pallas_call(
        _mm_slab_kernel,
        grid=(B, steps, N // tn),
        in_specs=[pl.BlockSpec((tm, K), lambda b, i, j: (b * steps + i, 0)),
                  pl.BlockSpec((K, tn), lambda b, i, j: (0, j)),
                  pl.BlockSpec((1, tn), lambda b, i, j: (0, j))],
        out_specs=pl.BlockSpec((None, tn // LANES, tm, LANES), lambda b, i, j: (b, j, i, 0)),
        out_shape=jax.ShapeDtypeStruct((B, N // LANES, S, LANES), F32),
        compiler_params=_cparams(("parallel", "parallel", "arbitrary")),
        name="indexer_proj",
    )(a, b, col_scale.reshape(1, N))


def _layer_norm(y, g, b):
    mu = jnp.mean(y, axis=-1, keepdims=True)
    yc = y - mu
    var = jnp.mean(jnp.square(yc), axis=-1, keepdims=True)
    return yc * lax.rsqrt(var + LN_EPS) * g + b


def _proj_ln_kernel(*refs, n_in):
    a_refs, w_refs = refs[:n_in], refs[n_in:2 * n_in]
    x_ref, g_ref, b_ref, y_ref, yb_ref = refs[2 * n_in:]
    acc = _dot(a_refs[0][...], w_refs[0][...])
    for a_ref, w_ref in zip(a_refs[1:], w_refs[1:]):
        acc = acc + _dot(a_ref[...], w_ref[...])
    y = _layer_norm(DN_ALPHA * x_ref[...] + acc, g_ref[...], b_ref[...])
    y_ref[...] = y
    yb_ref[...] = y.astype(BF16)


def proj_residual_ln(a_list, w, layer, x, g, b, tm=512):
    M, D = x.shape
    tm = min(tm, M)
    assert M % tm == 0
    n_in = len(a_list)
    kw = a_list[0].shape[1]
    assert all(a.shape[1] == kw for a in a_list) and n_in * kw == w.shape[1]
    in_specs = ([pl.BlockSpec((tm, kw), lambda i: (i, 0)) for _ in a_list]
                + [pl.BlockSpec((None, kw, D), lambda i, r=r: (layer, r, 0)) for r in range(n_in)]
                + [pl.BlockSpec((tm, D), lambda i: (i, 0)),
                   pl.BlockSpec((1, D), lambda i: (0, 0)),
                   pl.BlockSpec((1, D), lambda i: (0, 0))])
    return pl.pallas_call(
        functools.partial(_proj_ln_kernel, n_in=n_in),
        grid=(M // tm,),
        in_specs=in_specs,
        out_specs=[pl.BlockSpec((tm, D), lambda i: (i, 0)),
                   pl.BlockSpec((tm, D), lambda i: (i, 0))],
        out_shape=[jax.ShapeDtypeStruct((M, D), F32), jax.ShapeDtypeStruct((M, D), BF16)],
        compiler_params=_cparams(("parallel",)),
        name="out_proj_ln",
    )(*a_list, *([w] * n_in), x, g.reshape(1, D), b.reshape(1, D))


def _ffn_kernel(xb_ref, x_ref, w1_ref, w2_ref, g_ref, b_ref, y_ref, yb_ref, acc_ref):
    f = pl.program_id(1)

    @pl.when(f == 0)
    def _():
        acc_ref[...] = jnp.zeros_like(acc_ref)

    h = _dot(xb_ref[...], w1_ref[...])
    h = jnp.square(jnp.maximum(h, 0.0)).astype(BF16)
    acc_ref[...] += _dot(h, w2_ref[...])

    @pl.when(f == pl.num_programs(1) - 1)
    def _():
        y = _layer_norm(DN_ALPHA * x_ref[...] + acc_ref[...], g_ref[...], b_ref[...])
        y_ref[...] = y
        yb_ref[...] = y.astype(BF16)


def ffn_residual_ln(x, xb, w1, w2, layer, g, b, tm=512, tf=1024):
    M, D = x.shape
    F = w1.shape[2]
    tm, tf = min(tm, M), min(tf, F)
    assert M % tm == 0 and F % tf == 0
    return pl.pallas_call(
        _ffn_kernel,
        grid=(M // tm, F // tf),
        in_specs=[pl.BlockSpec((tm, D), lambda i, f: (i, 0)),
                  pl.BlockSpec((tm, D), lambda i, f: (i, 0)),
                  pl.BlockSpec((None, D, tf), lambda i, f: (layer, 0, f)),
                  pl.BlockSpec((None, tf, D), lambda i, f: (layer, f, 0)),
                  pl.BlockSpec((1, D), lambda i, f: (0, 0)),
                  pl.BlockSpec((1, D), lambda i, f: (0, 0))],
        out_specs=[pl.BlockSpec((tm, D), lambda i, f: (i, 0)),
                   pl.BlockSpec((tm, D), lambda i, f: (i, 0))],
        out_shape=[jax.ShapeDtypeStruct((M, D), F32), jax.ShapeDtypeStruct((M, D), BF16)],
        scratch_shapes=[pltpu.VMEM((tm, D), F32)],
        compiler_params=_cparams(("parallel", "arbitrary")),
        name="ffn_ln",
    )(xb, x, w1, w2, g.reshape(1, D), b.reshape(1, D))


def _sb_kernel(q_ref, k_ref, v_ref, o_ref, *, tq, tk, sub):
    i = pl.program_id(2)
    q = q_ref[...]
    r = lax.broadcasted_iota(I32, (sub, sub), 0)
    c = lax.broadcasted_iota(I32, (sub, sub), 1)
    tri = jnp.where(r > c, 1.0, 0.0).astype(BF16)
    qpos = i * tq + lax.broadcasted_iota(I32, (tq, tk), 0)
    koff = lax.broadcasted_iota(I32, (tq, tk), 1)

    def chunk(start, width, carry, acc, diag):
        k = k_ref[pl.ds(start, width), :]
        v = v_ref[pl.ds(start, width), :]
        z = _dot_nt(q, k)
        neg_abs = pltpu.bitcast(pltpu.bitcast(z, I32) | INT_MIN, F32)
        softplus = jnp.log(1.0 + jnp.exp2(neg_abs)) * LOG2E
        log_beta = jnp.minimum(z, 0.0) - softplus
        log_keep = log_beta - z
        if diag:
            strict = (start + koff) < qpos
            log_keep = jnp.where(strict, log_keep, 0.0)
        laters = []
        for s in reversed(range(width // sub)):
            lk = log_keep[:, s * sub:(s + 1) * sub]
            later = _dot(lk.astype(BF16), tri) + carry
            laters.append(later)
            carry = later[:, 0:1] + lk[:, 0:1]
        later = jnp.concatenate(laters[::-1], axis=1)
        a = jnp.exp2(log_beta + later)
        if diag:
            a = jnp.where(strict, a, 0.0)
        acc = acc + _dot(a.astype(BF16), v)
        return carry, acc

    first = pl.multiple_of((i * tq + tq - 1) // tk * tk, tk)
    carry = jnp.zeros((tq, 1), F32)
    acc = jnp.zeros((tq, q.shape[1]), F32)
    carry, acc = chunk(first, tk, carry, acc, True)

    def live(st):
        return (st[0] < first // sub) & (jnp.max(st[1]) >= F32_EXP2_ZERO)

    def step(st):
        start = pl.multiple_of(first - (st[0] + 1) * sub, sub)
        carry, acc = chunk(start, sub, st[1], st[2], False)
        return st[0] + 1, carry, acc

    _, carry, acc = lax.while_loop(live, step, (jnp.int32(0), carry, acc))
    o_ref[...] = acc.astype(o_ref.dtype)


def stick_breaking(h, B, S, q_col, k_col, v_col, n_heads, tq=512, tk=512, sub=256):
    dh = HEAD_DIM_SB
    tq, tk = min(tq, S), min(tk, S)
    sub = min(sub, tk)
    assert S % tq == 0 and S % tk == 0 and tk % sub == 0 and tk % tq == 0
    return pl.pallas_call(
        functools.partial(_sb_kernel, tq=tq, tk=tk, sub=sub),
        grid=(B, n_heads, S // tq),
        in_specs=[pl.BlockSpec((None, tq, dh), lambda b, hh, i: (b, i, q_col + hh)),
                  pl.BlockSpec((None, S, dh), lambda b, hh, i: (b, 0, k_col + hh)),
                  pl.BlockSpec((None, S, dh), lambda b, hh, i: (b, 0, v_col + hh))],
        out_specs=pl.BlockSpec((None, tq, dh), lambda b, hh, i: (b, i, hh)),
        out_shape=jax.ShapeDtypeStruct((B, S, n_heads * dh), BF16),
        compiler_params=_cparams(("parallel", "parallel", "arbitrary")),
        name="stick_breaking",
    )(h, h, h)


def _sel_kernel(qi_ref, ki_ref, wi_ref, o_ref, keys_ref, *, blk, nk, ch, topk, w_scale):
    n = pl.program_id(1)
    nh = qi_ref.shape[0]
    cw = ch * blk
    last = n // ch
    qall = qi_ref[...].reshape(nh * blk, LANES).astype(BF16)
    w8 = wi_ref[...].T[:nh, :] * w_scale
    row = lax.broadcasted_iota(I32, (blk, blk), 0)
    col = lax.broadcasted_iota(I32, (blk, blk), 1)
    tri = jnp.where(col < row, 1.0, 0.0).astype(BF16)
    qpos = n * blk + lax.broadcasted_iota(I32, (cw, blk), 1)
    koff = lax.broadcasted_iota(I32, (cw, blk), 0)

    def score_keys(cc, diag):
        start = pl.multiple_of(cc * cw, cw)
        d = _dot_nt(ki_ref[pl.ds(start, cw), :].astype(BF16), qall)
        d = jnp.maximum(d, 0.0)
        sc = d[:, 0:blk] * w8[0:1, :]
        for hh in range(1, nh):
            sc = sc + d[:, hh * blk:(hh + 1) * blk] * w8[hh:hh + 1, :]
        sc = jnp.where(sc == 0.0, 0.0, sc)
        bits = pltpu.bitcast(sc, I32)
        key = jnp.where(bits < 0, bits ^ 0x7FFFFFFF, bits)
        if diag:
            key = jnp.where(start + koff <= qpos, key, INT_MIN)
        keys_ref[pl.ds(cc * ch, ch)] = key.reshape(ch, blk, blk)

    score_keys(last, True)
    lax.fori_loop(0, last, lambda cc, c: (score_keys(cc, False), c)[1], 0)

    def count(pred):
        def body(cc, c):
            kk = keys_ref[pl.ds(cc * ch, ch)]
            for j in range(ch):
                c = c + jnp.where(pred(kk[j]), 1, 0)
            return c
        c = lax.fori_loop(0, last + 1, body, jnp.zeros((blk, blk), I32))
        return jnp.sum(c, axis=0, keepdims=True)

    c0 = count(lambda key: key >= 0)
    theta = jnp.where(c0 >= topk, 0, INT_MIN).astype(I32)

    def bit_step(i, theta):
        cand = theta + lax.shift_left(jnp.int32(1), 30 - i)
        c = count(lambda key: key >= cand)
        return jnp.where(c >= topk, cand, theta)

    theta = lax.fori_loop(0, 31, bit_step, theta)
    need = (topk - count(lambda key: key > theta)).astype(F32)

    def emit(cc, carry, diag):
        kk = keys_ref[pl.ds(cc * ch, ch)]
        for j in range(ch):
            key = kk[j]
            eqf = jnp.where(key == theta, 1.0, 0.0)
            rank = _dot(tri, eqf.astype(BF16)) + carry
            sel = jnp.where(key > theta, 1.0, jnp.where(rank < need, eqf, 0.0))
            if diag:
                kb = cc * ch + j
                sel = jnp.where(kb * blk + row <= n * blk + col, sel, 0.0)
            madd = jnp.where(sel > 0.5, 0.0, -jnp.inf)
            o_ref[cc * ch + j] = madd.T.astype(o_ref.dtype)
            carry = carry + jnp.sum(eqf, axis=0, keepdims=True)
        return carry

    carry = lax.fori_loop(0, last, lambda cc, c: emit(cc, c, False), jnp.zeros((1, blk), F32))
    emit(last, carry, True)

    def fill(kb, c):
        o_ref[kb] = jnp.full((blk, blk), -jnp.inf, o_ref.dtype)
        return c

    lax.fori_loop((last + 1) * ch, nk, fill, 0)


def dsa_select(hi, topk, ch=4):
    B, nslab, S, _ = hi.shape
    nh = N_HEADS_IDX
    assert nslab == nh + 2
    blk = Q_BLOCK
    nq = S // blk
    ch = min(ch, nq)
    assert nq % ch == 0
    return pl.pallas_call(
        functools.partial(_sel_kernel, blk=blk, nk=nq, ch=ch, topk=topk, w_scale=N_HEADS_IDX ** -0.5),
        grid=(B, nq),
        in_specs=[pl.BlockSpec((None, nh, blk, LANES), lambda b, i: (b, 0, i, 0)),
                  pl.BlockSpec((None, None, S, LANES), lambda b, i: (b, nh, 0, 0)),
                  pl.BlockSpec((None, None, blk, LANES), lambda b, i: (b, nh + 1, i, 0))],
        out_specs=pl.BlockSpec((None, None, nq, blk, blk), lambda b, i: (b, i, 0, 0, 0)),
        out_shape=jax.ShapeDtypeStruct((B, nq, nq, blk, blk), BF16),
        scratch_shapes=[pltpu.VMEM((nq, blk, blk), I32)],
        compiler_params=_cparams(("parallel", "arbitrary")),
        name="dsa_select",
    )(hi, hi, hi)


def _dsa_kernel(q_ref, k_ref, v_ref, m_ref, vec_ref, o_ref, bias_ref, *, tq, tk, dh):
    i = pl.program_id(2)
    blk = m_ref.shape[-1]
    rq, rk = tq // blk, tk // blk
    nb = bias_ref.shape[1] - 1

    @pl.when((i == 0) & (pl.program_id(1) == 0))
    def _():
        def expand(kk, c):
            for hh in range(2):
                rows = jnp.broadcast_to(vec_ref[hh, pl.ds(kk, 1), :], (blk, 2 * blk))
                bias_ref[hh, kk] = pltpu.roll(rows, 0, 1, stride=1, stride_axis=0)[:, :blk]
            return c
        lax.fori_loop(0, nb + 1, expand, 0)

    q = q_ref[...]
    lane = lax.broadcasted_iota(I32, q.shape, 1)
    halves = (lane < dh, lane >= dh)
    qs = [jnp.where(hm, q, jnp.zeros_like(q)) for hm in halves]
    def step(kt0, nkt, r0, carry):
        def tiles(load):
            return jnp.concatenate(
                [jnp.concatenate([load(r, c) for c in range(nkt)], axis=1) for r in range(r0, rq)], axis=0)

        start = pl.multiple_of(kt0 * blk, blk)
        k = k_ref[pl.ds(start, nkt * blk), :]
        v = v_ref[pl.ds(start, nkt * blk), :]
        vlane = lax.broadcasted_iota(I32, v.shape, 1)
        madd = tiles(lambda r, c: m_ref[r, kt0 + c]).astype(F32)
        out = []
        for hh in range(2):
            m_all, acc_all = carry[hh]
            m, acc = m_all[r0 * blk:], acc_all[r0 * blk:]
            bias = tiles(lambda r, c: bias_ref[hh, jnp.minimum(nb - (i * rq + r) + (kt0 + c), nb)])
            s = _dot_nt(qs[hh][r0 * blk:], k) + bias + madd
            m_new = jnp.maximum(m, jnp.max(s, axis=1, keepdims=True))
            p = jnp.exp2(s - m_new)
            vh = jnp.where((vlane < dh) if hh == 0 else (vlane >= dh), v, jnp.ones_like(v))
            acc = acc * jnp.exp2(m - m_new) + _dot(p.astype(BF16), vh)
            if r0:
                m_new = jnp.concatenate([m_all[:r0 * blk], m_new], axis=0)
                acc = jnp.concatenate([acc_all[:r0 * blk], acc], axis=0)
            out.append((m_new, acc))
        return tuple(out)

    init = tuple((jnp.full((tq, 1), NEG, F32), jnp.zeros((tq, LANES), F32)) for _ in range(2))
    carry = lax.fori_loop(0, i, lambda kc, c: step(kc * rk, rk, 0, c), init)
    carry = step(i * rk, rk // 2, 0, carry)
    (_, a0), (_, a1) = step(i * rk + rk // 2, rk // 2, rq // 2, carry)
    o_ref[...] = jnp.where(halves[0], a0 / a0[:, dh:dh + 1], a1 / a1[:, 0:1]).astype(o_ref.dtype)


def dsa_attention(h, madd, bias_vecs, B, S, q_col, k_col, v_col, tq=1024, tk=1024):
    blk = Q_BLOCK
    nq = S // blk
    tq, tk = min(tq, S), min(tk, S)
    assert S % tq == 0 and tq == tk and tq % (2 * blk) == 0
    n_pairs = N_HEADS_DSA * HEAD_DIM_DSA // LANES
    assert bias_vecs.shape == (2 * n_pairs, nq + 1, 2 * blk)
    return pl.pallas_call(
        functools.partial(_dsa_kernel, tq=tq, tk=tk, dh=HEAD_DIM_DSA),
        grid=(n_pairs, B, S // tq),
        in_specs=[pl.BlockSpec((None, tq, LANES), lambda p, b, i: (b, i, q_col + p)),
                  pl.BlockSpec((None, S, LANES), lambda p, b, i: (b, 0, k_col + p)),
                  pl.BlockSpec((None, S, LANES), lambda p, b, i: (b, 0, v_col + p)),
                  pl.BlockSpec((None, tq // blk, nq, blk, blk), lambda p, b, i: (b, i, 0, 0, 0)),
                  pl.BlockSpec((2, nq + 1, 2 * blk), lambda p, b, i: (p, 0, 0))],
        out_specs=pl.BlockSpec((None, tq, LANES), lambda p, b, i: (b, i, p)),
        out_shape=jax.ShapeDtypeStruct((B, S, n_pairs * LANES), BF16),
        scratch_shapes=[pltpu.VMEM((2, nq + 1, blk, blk), F32)],
        compiler_params=_cparams(("parallel", "arbitrary", "arbitrary")),
        name="dsa_attention",
    )(h, h, h, madd, bias_vecs)


def _dil_kernel(q_ref, k_ref, v_ref, bias_ref, o_ref, qf, kf, vf, m_s, l_s, acc_s,
                *, blk, chunk, dils, unroll):
    c = pl.program_id(2)
    base = pl.multiple_of(c * chunk, chunk)

    @pl.when(c == 0)
    def _():
        qf[...] = q_ref[...].astype(F32)
        kf[...] = k_ref[...].astype(F32)
        vf[...] = v_ref[...].astype(F32)

    for g, d in enumerate(dils):
        span = blk * d

        def group(it, carry, g=g, d=d, span=span):
            for jj in range(unroll):
                t = it * unroll + jj
                u = t // d
                r = t - u * d
                start = base + u * span + r
                has_prev = start >= span
                prev = jnp.where(has_prev, start - span, start)
                if d == 1:
                    start = pl.multiple_of(start, blk)
                    prev = pl.multiple_of(prev, blk)
                    rows, prows = pl.ds(start, blk), pl.ds(prev, blk)
                    lrows = pl.ds(pl.multiple_of(start - base, blk), blk)
                else:
                    rows, prows = pl.ds(start, blk, stride=d), pl.ds(prev, blk, stride=d)
                    lrows = pl.ds(start - base, blk, stride=d)
                q = qf[rows, :].astype(BF16)
                kk = jnp.concatenate([kf[prows, :], kf[rows, :]], axis=0).astype(BF16)
                vv = jnp.concatenate([vf[prows, :], vf[rows, :]], axis=0).astype(BF16)
                s = _dot_nt(q, kk) + bias_ref[g, jnp.where(has_prev, 0, 1)]
                m_t = jnp.max(s, axis=1, keepdims=True)
                e = jnp.exp2(s - m_t)
                l_t = jnp.sum(e, axis=1, keepdims=True)
                m_s[g, lrows, :] = jnp.broadcast_to(m_t, (blk, LANES))
                l_s[g, lrows, :] = jnp.broadcast_to(l_t, (blk, LANES))
                acc_s[g, lrows, :] = _dot(e.astype(BF16), vv)
            return carry

        lax.fori_loop(0, chunk // (blk * unroll), group, 0)

    m = m_s[0]
    for g in range(1, len(dils)):
        m = jnp.maximum(m, m_s[g])
    num = den = None
    for g in range(len(dils)):
        w = jnp.exp2(m_s[g] - m)
        num = acc_s[g] * w if num is None else num + acc_s[g] * w
        den = l_s[g] * w if den is None else den + l_s[g] * w
    o_ref[...] = (num / den).astype(o_ref.dtype)


def dilated_attention(h, bias_tiles, B, S, q_col, k_col, v_col, n_heads, unroll=16):
    blk = Q_BLOCK
    dils = tuple(d for _, d in DILATED_CONFIGS)
    assert all(w // d == blk for w, d in DILATED_CONFIGS)
    chunk = blk * max(dils)
    assert S % chunk == 0 and (chunk // blk) % unroll == 0
    dh = HEAD_DIM_DIL
    ng = len(dils)
    return pl.pallas_call(
        functools.partial(_dil_kernel, blk=blk, chunk=chunk, dils=dils, unroll=unroll),
        grid=(B, n_heads, S // chunk),
        in_specs=[pl.BlockSpec((None, S, dh), lambda b, hh, c: (b, 0, q_col + hh)),
                  pl.BlockSpec((None, S, dh), lambda b, hh, c: (b, 0, k_col + hh)),
                  pl.BlockSpec((None, S, dh), lambda b, hh, c: (b, 0, v_col + hh)),
                  pl.BlockSpec((None, ng, 2, blk, 2 * blk), lambda b, hh, c: (hh, 0, 0, 0, 0))],
        out_specs=pl.BlockSpec((None, chunk, dh), lambda b, hh, c: (b, c, hh)),
        out_shape=jax.ShapeDtypeStruct((B, S, n_heads * dh), BF16),
        scratch_shapes=[pltpu.VMEM((S, dh), F32), pltpu.VMEM((S, dh), F32), pltpu.VMEM((S, dh), F32),
                        pltpu.VMEM((ng, chunk, LANES), F32), pltpu.VMEM((ng, chunk, LANES), F32),
                        pltpu.VMEM((ng, chunk, dh), F32)],
        compiler_params=_cparams(("parallel", "parallel", "arbitrary")),
        name="dilated_attention",
    )(h, h, h, bias_tiles)


def _rel_bucket(dist):
    max_exact = N_BUCKETS // 2
    d_f = jnp.maximum(dist, 1).astype(F32)
    large = max_exact + (jnp.log(d_f / max_exact) / math.log(BUCKET_MAX_DIST / max_exact)
                         * (N_BUCKETS - max_exact)).astype(I32)
    large = jnp.minimum(large, N_BUCKETS - 1)
    return jnp.where(dist < max_exact, dist, large)


def _bias_by_distance(rel_bias, n):
    return rel_bias.astype(F32)[_rel_bucket(jnp.arange(n, dtype=I32))].T


def _toeplitz_band(g, nblk):
    blk = Q_BLOCK
    period = blk * (nblk + 2)
    m = jnp.arange(period, dtype=I32)
    m = jnp.where(m >= period - blk, m - period, m)
    v = g[:, jnp.clip(blk * nblk - m, 0, g.shape[1] - 1)]
    flat = jnp.tile(v, (1, blk + 1))[:, :blk * (period - 1)]
    return flat.reshape(g.shape[0], blk, period - 1)[:, :, :blk * (nblk + 1)]


def _causal_bias_vectors(rel_bias, nq):
    blk = Q_BLOCK
    g = _bias_by_distance(rel_bias, nq * blk) * LOG2E
    m = np.arange(2 * blk)
    m = np.where(m >= blk, m - 2 * blk, m)
    dist = blk * (nq - np.arange(nq + 1))[:, None] - m[None, :]
    return g[:, np.clip(dist, 0, nq * blk - 1)]


def _dilated_bias_tiles(rel_bias):
    blk = Q_BLOCK
    g = _bias_by_distance(rel_bias, 2 * blk * max(d for _, d in DILATED_CONFIGS)) * LOG2E
    j = blk + np.arange(blk)[:, None] - np.arange(2 * blk)[None, :]
    band = (j >= 0) & (j <= blk)
    masks = np.stack([band, band & (np.arange(2 * blk)[None, :] >= blk)])
    tiles = jnp.stack([_toeplitz_band(g[:, ::d][:, :2 * blk], 1) for _, d in DILATED_CONFIGS], axis=1)
    return jnp.where(masks[None, None], tiles[:, :, None], NEG)


def _lane_slabs(w, width):
    K, N = w.shape
    w = w.reshape(K, N // width, width)
    return jnp.pad(w, ((0, 0), (0, 0), (0, LANES - width))).reshape(K, N // width * LANES)


def even_mixer(x, w_in, idx, rel_bias, B, S):
    half = N_HEADS_SB * HEAD_DIM_SB
    n_main = 6 * half
    n_qi = N_HEADS_IDX * HEAD_DIM_IDX
    slab = half // LANES
    col_scale = _col_scale(n_main, [(0, half, HEAD_DIM_SB ** -0.5 * LOG2E),
                                    (3 * half, 4 * half, HEAD_DIM_DSA ** -0.5 * LOG2E)])
    h = matmul(x, w_in, idx, col_scale, BF16).reshape(B, S, n_main)
    w_idx = w_in[idx, :, n_main:]
    w_idx = jnp.concatenate([_lane_slabs(w_idx[:, :n_qi], HEAD_DIM_IDX),
                             _lane_slabs(w_idx[:, n_qi:n_qi + HEAD_DIM_IDX], HEAD_DIM_IDX),
                             _lane_slabs(w_idx[:, n_qi + HEAD_DIM_IDX:], N_HEADS_IDX)], axis=1)
    idx_scale = _col_scale(w_idx.shape[1], [(0, N_HEADS_IDX * LANES, HEAD_DIM_IDX ** -0.5)])
    hi = matmul_slabs(x, w_idx.astype(BF16), idx_scale, B, S)
    madd = dsa_select(hi, min(TOPK_MAX, S // 4))
    oa = stick_breaking(h, B, S, 0, slab, 2 * slab, N_HEADS_SB)
    ob = dsa_attention(h, madd, _causal_bias_vectors(rel_bias, S // Q_BLOCK), B, S,
                       3 * slab, 4 * slab, 5 * slab)
    return [oa.reshape(B * S, half), ob.reshape(B * S, half)]


def odd_mixer(xb, w_in, idx, rel_bias, B, S):
    width = N_HEADS_DIL * HEAD_DIM_DIL
    col_scale = _col_scale(3 * width, [(0, width, HEAD_DIM_DIL ** -0.5 * LOG2E)])
    h = matmul(xb, w_in, idx, col_scale, BF16).reshape(B, S, 3 * width)
    o = dilated_attention(h, _dilated_bias_tiles(rel_bias), B, S, 0, N_HEADS_DIL, 2 * N_HEADS_DIL,
                          N_HEADS_DIL)
    return [o.reshape(B * S, width)]


def kernel(x, even_w_in, even_w_out, odd_w_in, odd_w_out, rel_bias, ln_mix_g, ln_mix_b,
           ffn_w1, ffn_w2, ln_ffn_g, ln_ffn_b):
    B, S, D = x.shape
    x2 = x.reshape(B * S, D)
    xb = x2
    w_out = (even_w_out.astype(BF16), odd_w_out.astype(BF16))
    w1, w2 = ffn_w1.astype(BF16), ffn_w2.astype(BF16)
    for layer in range(DEPTH):
        idx = layer // 2
        if layer % 2 == 0:
            a_list = even_mixer(xb, even_w_in, idx, rel_bias, B, S)
        else:
            a_list = odd_mixer(xb, odd_w_in, idx, rel_bias, B, S)
        x2, xb = proj_residual_ln(a_list, w_out[layer % 2], idx, x2, ln_mix_g[layer], ln_mix_b[layer])
        x2, xb = ffn_residual_ln(x2, xb, w1, w2, layer, ln_ffn_g[layer], ln_ffn_b[layer])
    return x2.reshape(B, S, D)
```

```python
import functools
import math

import jax
import jax.numpy as jnp
import numpy as np
from jax import lax
from jax.experimental import pallas as pl
from jax.experimental.pallas import tpu as pltpu

F32 = jnp.float32
BF16 = jnp.bfloat16
I32 = jnp.int32

Q_BLOCK = 128
HEAD_DIM_SB = 128
N_HEADS_SB = 8
N_HEADS_DSA = 16
HEAD_DIM_DSA = 64
N_HEADS_IDX = 8
HEAD_DIM_IDX = 64
TOPK_MAX = 256
N_HEADS_DIL = 16
HEAD_DIM_DIL = 128
DILATED_CONFIGS = ((128, 1), (512, 4), (2048, 16))
N_BUCKETS = 32
BUCKET_MAX_DIST = 2048
DEPTH = 2
DN_ALPHA = (2 * DEPTH) ** 0.25
LN_EPS = 1e-5
NEG = -1e30
INT_MIN = -(2 ** 31)
LOG2E = math.log2(math.e)
F32_EXP2_ZERO = -150.0

LANES = 128
VMEM_LIMIT_BYTES = 52 * 1024 * 1024


def _cparams(sem):
    return pltpu.CompilerParams(dimension_semantics=sem, vmem_limit_bytes=VMEM_LIMIT_BYTES)


def _dot(a, b):
    return jnp.dot(a, b, preferred_element_type=F32)


def _dot_nt(a, b):
    return lax.dot_general(a, b, (((1,), (1,)), ((), ())), preferred_element_type=F32)


def _mm_kernel(a_ref, w_ref, s_ref, o_ref, wb_ref):
    @pl.when(pl.program_id(1) == 0)
    def _():
        wb_ref[...] = w_ref[...].astype(BF16)

    acc = _dot(a_ref[...].astype(BF16), wb_ref[...])
    o_ref[...] = (acc * s_ref[...]).astype(o_ref.dtype)


def matmul(a, w, layer, col_scale, out_dtype, tm=1024, tn=1024):
    M, K = a.shape
    N = col_scale.shape[0]
    tm, tn = min(tm, M), min(tn, N)
    assert M % tm == 0 and N % tn == 0 and N <= w.shape[2]
    return pl.pallas_call(
        _mm_kernel,
        grid=(N // tn, M // tm),
        in_specs=[pl.BlockSpec((tm, K), lambda j, i: (i, 0)),
                  pl.BlockSpec((None, K, tn), lambda j, i: (layer, 0, j)),
                  pl.BlockSpec((1, tn), lambda j, i: (0, j))],
        out_specs=pl.BlockSpec((tm, tn), lambda j, i: (i, j)),
        out_shape=jax.ShapeDtypeStruct((M, N), out_dtype),
        scratch_shapes=[pltpu.VMEM((K, tn), BF16)],
        compiler_params=_cparams(("parallel", "arbitrary")),
        name="proj_matmul",
    )(a, w, col_scale.reshape(1, N))


def _col_scale(n, scaled):
    s = np.ones((n,), np.float32)
    for lo, hi, f in scaled:
        s[lo:hi] = f
    return jnp.asarray(s)


def _mm_slab_kernel(a_ref, b_ref, s_ref, o_ref):
    acc = _dot(a_ref[...].astype(BF16), b_ref[...]) * s_ref[...]
    for j in range(o_ref.shape[0]):
        o_ref[j] = acc[:, j * LANES:(j + 1) * LANES]


def matmul_slabs(a, b, col_scale, B, S, tm=1024, tn=1280):
    M, K = a.shape
    N = b.shape[1]
    tm, tn = min(tm, S), min(tn, N)
    assert M == B * S and S % tm == 0 and N % tn == 0 and tn % LANES == 0
    steps = S // tm
    return pl.pallas_call(
        _mm_slab_kernel,
        grid=(B, steps, N // tn),
        in_specs=[pl.BlockSpec((tm, K), lambda b, i, j: (b * steps + i, 0)),
                  pl.BlockSpec((K, tn), lambda b, i, j: (0, j)),
                  pl.BlockSpec((1, tn), lambda b, i, j: (0, j))],
        out_specs=pl.BlockSpec((None, tn // LANES, tm, LANES), lambda b, i, j: (b, j, i, 0)),
        out_shape=jax.ShapeDtypeStruct((B, N // LANES, S, LANES), F32),
        compiler_params=_cparams(("parallel", "parallel", "arbitrary")),
        name="indexer_proj",
    )(a, b, col_scale.reshape(1, N))


def _layer_norm(y, g, b):
    mu = jnp.mean(y, axis=-1, keepdims=True)
    yc = y - mu
    var = jnp.mean(jnp.square(yc), axis=-1, keepdims=True)
    return yc * lax.rsqrt(var + LN_EPS) * g + b


def _proj_ln_kernel(*refs, n_in):
    a_refs, w_refs = refs[:n_in], refs[n_in:2 * n_in]
    x_ref, g_ref, b_ref, y_ref, yb_ref = refs[2 * n_in:]
    acc = _dot(a_refs[0][...], w_refs[0][...])
    for a_ref, w_ref in zip(a_refs[1:], w_refs[1:]):
        acc = acc + _dot(a_ref[...], w_ref[...])
    y = _layer_norm(DN_ALPHA * x_ref[...] + acc, g_ref[...], b_ref[...])
    y_ref[...] = y
    yb_ref[...] = y.astype(BF16)


def proj_residual_ln(a_list, w, layer, x, g, b, tm=512):
    M, D = x.shape
    tm = min(tm, M)
    assert M % tm == 0
    n_in = len(a_list)
    kw = a_list[0].shape[1]
    assert all(a.shape[1] == kw for a in a_list) and n_in * kw == w.shape[1]
    in_specs = ([pl.BlockSpec((tm, kw), lambda i: (i, 0)) for _ in a_list]
                + [pl.BlockSpec((None, kw, D), lambda i, r=r: (layer, r, 0)) for r in range(n_in)]
                + [pl.BlockSpec((tm, D), lambda i: (i, 0)),
                   pl.BlockSpec((1, D), lambda i: (0, 0)),
                   pl.BlockSpec((1, D), lambda i: (0, 0))])
    return pl.pallas_call(
        functools.partial(_proj_ln_kernel, n_in=n_in),
        grid=(M // tm,),
        in_specs=in_specs,
        out_specs=[pl.BlockSpec((tm, D), lambda i: (i, 0)),
                   pl.BlockSpec((tm, D), lambda i: (i, 0))],
        out_shape=[jax.ShapeDtypeStruct((M, D), F32), jax.ShapeDtypeStruct((M, D), BF16)],
        compiler_params=_cparams(("parallel",)),
        name="out_proj_ln",
    )(*a_list, *([w] * n_in), x, g.reshape(1, D), b.reshape(1, D))


def _ffn_kernel(xb_ref, x_ref, w1_ref, w2_ref, g_ref, b_ref, y_ref, yb_ref, acc_ref):
    f = pl.program_id(1)

    @pl.when(f == 0)
    def _():
        acc_ref[...] = jnp.zeros_like(acc_ref)

    h = _dot(xb_ref[...], w1_ref[...])
    h = jnp.square(jnp.maximum(h, 0.0)).astype(BF16)
    acc_ref[...] += _dot(h, w2_ref[...])

    @pl.when(f == pl.num_programs(1) - 1)
    def _():
        y = _layer_norm(DN_ALPHA * x_ref[...] + acc_ref[...], g_ref[...], b_ref[...])
        y_ref[...] = y
        yb_ref[...] = y.astype(BF16)


def ffn_residual_ln(x, xb, w1, w2, layer, g, b, tm=512, tf=1024):
    M, D = x.shape
    F = w1.shape[2]
    tm, tf = min(tm, M), min(tf, F)
    assert M % tm == 0 and F % tf == 0
    return pl.pallas_call(
        _ffn_kernel,
        grid=(M // tm, F // tf),
        in_specs=[pl.BlockSpec((tm, D), lambda i, f: (i, 0)),
                  pl.BlockSpec((tm, D), lambda i, f: (i, 0)),
                  pl.BlockSpec((None, D, tf), lambda i, f: (layer, 0, f)),
                  pl.BlockSpec((None, tf, D), lambda i, f: (layer, f, 0)),
                  pl.BlockSpec((1, D), lambda i, f: (0, 0)),
                  pl.BlockSpec((1, D), lambda i, f: (0, 0))],
        out_specs=[pl.BlockSpec((tm, D), lambda i, f: (i, 0)),
                   pl.BlockSpec((tm, D), lambda i, f: (i, 0))],
        out_shape=[jax.ShapeDtypeStruct((M, D), F32), jax.ShapeDtypeStruct((M, D), BF16)],
        scratch_shapes=[pltpu.VMEM((tm, D), F32)],
        compiler_params=_cparams(("parallel", "arbitrary")),
        name="ffn_ln",
    )(xb, x, w1, w2, g.reshape(1, D), b.reshape(1, D))


def _sb_kernel(q_ref, k_ref, v_ref, o_ref, *, tq, sub):
    i = pl.program_id(2)
    q = q_ref[...]
    r = lax.broadcasted_iota(I32, (sub, sub), 0)
    c = lax.broadcasted_iota(I32, (sub, sub), 1)
    tri = jnp.where(r > c, 1.0, 0.0).astype(BF16)

    def chunk(start, row0, carry_all, acc_all, diag):
        carry, acc = carry_all[row0:], acc_all[row0:]
        k = k_ref[pl.ds(start, sub), :]
        v = v_ref[pl.ds(start, sub), :]
        z = _dot_nt(q[row0:], k)
        neg_abs = pltpu.bitcast(pltpu.bitcast(z, I32) | INT_MIN, F32)
        softplus = jnp.log(1.0 + jnp.exp2(neg_abs)) * LOG2E
        log_beta = jnp.minimum(z, 0.0) - softplus
        log_keep = log_beta - z
        if diag:
            qpos = i * tq + row0 + lax.broadcasted_iota(I32, z.shape, 0)
            strict = (start + lax.broadcasted_iota(I32, z.shape, 1)) < qpos
            log_keep = jnp.where(strict, log_keep, 0.0)
        later = _dot(log_keep.astype(BF16), tri) + carry
        carry = later[:, 0:1] + log_keep[:, 0:1]
        a = jnp.exp2(log_beta + later)
        if diag:
            a = jnp.where(strict, a, 0.0)
        acc = acc + _dot(a.astype(BF16), v)
        if row0:
            carry = jnp.concatenate([carry_all[:row0], carry], axis=0)
            acc = jnp.concatenate([acc_all[:row0], acc], axis=0)
        return carry, acc

    first = pl.multiple_of(i * tq, tq)
    carry = jnp.zeros((tq, 1), F32)
    acc = jnp.zeros((tq, q.shape[1]), F32)
    for j in reversed(range(tq // sub)):
        carry, acc = chunk(pl.multiple_of(first + j * sub, sub), j * sub, carry, acc, True)

    def live(st):
        return (st[0] < first // sub) & (jnp.max(st[1]) >= F32_EXP2_ZERO)

    def step(st):
        start = pl.multiple_of(first - (st[0] + 1) * sub, sub)
        carry, acc = chunk(start, 0, st[1], st[2], False)
        return st[0] + 1, carry, acc

    _, carry, acc = lax.while_loop(live, step, (jnp.int32(0), carry, acc))
    o_ref[...] = acc.astype(o_ref.dtype)


def stick_breaking(h, B, S, q_col, k_col, v_col, n_heads, tq=512, sub=256):
    dh = HEAD_DIM_SB
    tq = min(tq, S)
    sub = min(sub, tq)
    assert S % tq == 0 and tq % sub == 0
    return pl.pallas_call(
        functools.partial(_sb_kernel, tq=tq, sub=sub),
        grid=(B, n_heads, S // tq),
        in_specs=[pl.BlockSpec((None, tq, dh), lambda b, hh, i: (b, i, q_col + hh)),
                  pl.BlockSpec((None, S, dh), lambda b, hh, i: (b, 0, k_col + hh)),
                  pl.BlockSpec((None, S, dh), lambda b, hh, i: (b, 0, v_col + hh))],
        out_specs=pl.BlockSpec((None, tq, dh), lambda b, hh, i: (b, i, hh)),
        out_shape=jax.ShapeDtypeStruct((B, S, n_heads * dh), BF16),
        compiler_params=_cparams(("parallel", "parallel", "arbitrary")),
        name="stick_breaking",
    )(h, h, h)


def _sel_kernel(qi_ref, ki_ref, wi_ref, o_ref, keys_ref, *, blk, nk, ch, topk, w_scale):
    n = pl.program_id(1)
    nh = qi_ref.shape[0]
    cw = ch * blk
    last = n // ch
    qall = qi_ref[...].reshape(nh * blk, LANES).astype(BF16)
    w8 = wi_ref[...].T[:nh, :] * w_scale
    row = lax.broadcasted_iota(I32, (blk, blk), 0)
    col = lax.broadcasted_iota(I32, (blk, blk), 1)
    tri = jnp.where(col < row, 1.0, 0.0).astype(BF16)
    qpos = n * blk + lax.broadcasted_iota(I32, (cw, blk), 1)
    koff = lax.broadcasted_iota(I32, (cw, blk), 0)

    def score_keys(cc, diag):
        start = pl.multiple_of(cc * cw, cw)
        d = _dot_nt(ki_ref[pl.ds(start, cw), :].astype(BF16), qall)
        d = jnp.maximum(d, 0.0)
        sc = d[:, 0:blk] * w8[0:1, :]
        for hh in range(1, nh):
            sc = sc + d[:, hh * blk:(hh + 1) * blk] * w8[hh:hh + 1, :]
        sc = jnp.where(sc == 0.0, 0.0, sc)
        bits = pltpu.bitcast(sc, I32)
        key = jnp.where(bits < 0, bits ^ 0x7FFFFFFF, bits)
        if diag:
            key = jnp.where(start + koff <= qpos, key, INT_MIN)
        keys_ref[pl.ds(cc * ch, ch)] = key.reshape(ch, blk, blk)

    score_keys(last, True)
    lax.fori_loop(0, last, lambda cc, c: (score_keys(cc, False), c)[1], 0)

    def count(pred):
        def body(cc, c):
            kk = keys_ref[pl.ds(cc * ch, ch)]
            for j in range(ch):
                c = c + jnp.where(pred(kk[j]), 1, 0)
            return c
        c = lax.fori_loop(0, last + 1, body, jnp.zeros((blk, blk), I32))
        return jnp.sum(c, axis=0, keepdims=True)

    c0 = count(lambda key: key >= 0)
    theta = jnp.where(c0 >= topk, 0, INT_MIN).astype(I32)

    def bit_step(i, theta):
        cand = theta + lax.shift_left(jnp.int32(1), 30 - i)
        c = count(lambda key: key >= cand)
        return jnp.where(c >= topk, cand, theta)

    theta = lax.fori_loop(0, 31, bit_step, theta)
    need = (topk - count(lambda key: key > theta)).astype(F32)

    def emit(cc, carry, diag):
        kk = keys_ref[pl.ds(cc * ch, ch)]
        for j in range(ch):
            key = kk[j]
            eqf = jnp.where(key == theta, 1.0, 0.0)
            rank = _dot(tri, eqf.astype(BF16)) + carry
            sel = jnp.where(key > theta, 1.0, jnp.where(rank < need, eqf, 0.0))
            if diag:
                kb = cc * ch + j
                sel = jnp.where(kb * blk + row <= n * blk + col, sel, 0.0)
            madd = jnp.where(sel > 0.5, 0.0, -jnp.inf)
            o_ref[cc * ch + j] = madd.T.astype(o_ref.dtype)
            carry = carry + jnp.sum(eqf, axis=0, keepdims=True)
        return carry

    carry = lax.fori_loop(0, last, lambda cc, c: emit(cc, c, False), jnp.zeros((1, blk), F32))
    emit(last, carry, True)

    def fill(kb, c):
        o_ref[kb] = jnp.full((blk, blk), -jnp.inf, o_ref.dtype)
        return c

    lax.fori_loop((last + 1) * ch, nk, fill, 0)


def dsa_select(hi, topk, ch=4):
    B, nslab, S, _ = hi.shape
    nh = N_HEADS_IDX
    assert nslab == nh + 2
    blk = Q_BLOCK
    nq = S // blk
    ch = min(ch, nq)
    assert nq % ch == 0
    return pl.pallas_call(
        functools.partial(_sel_kernel, blk=blk, nk=nq, ch=ch, topk=topk, w_scale=N_HEADS_IDX ** -0.5),
        grid=(B, nq),
        in_specs=[pl.BlockSpec((None, nh, blk, LANES), lambda b, i: (b, 0, i, 0)),
                  pl.BlockSpec((None, None, S, LANES), lambda b, i: (b, nh, 0, 0)),
                  pl.BlockSpec((None, None, blk, LANES), lambda b, i: (b, nh + 1, i, 0))],
        out_specs=pl.BlockSpec((None, None, nq, blk, blk), lambda b, i: (b, i, 0, 0, 0)),
        out_shape=jax.ShapeDtypeStruct((B, nq, nq, blk, blk), BF16),
        scratch_shapes=[pltpu.VMEM((nq, blk, blk), I32)],
        compiler_params=_cparams(("parallel", "arbitrary")),
        name="dsa_select",
    )(hi, hi, hi)


def _dsa_kernel(q_ref, k_ref, v_ref, m_ref, vec_ref, o_ref, bias_ref, *, tq, tk, dh):
    i = pl.program_id(2)
    blk = m_ref.shape[-1]
    rq, rk = tq // blk, tk // blk
    nb = bias_ref.shape[1] - 1

    @pl.when((i == 0) & (pl.program_id(1) == 0))
    def _():
        def expand(kk, c):
            for hh in range(2):
                rows = jnp.broadcast_to(vec_ref[hh, pl.ds(kk, 1), :], (blk, 2 * blk))
                bias_ref[hh, kk] = pltpu.roll(rows, 0, 1, stride=1, stride_axis=0)[:, :blk]
            return c
        lax.fori_loop(0, nb + 1, expand, 0)

    q = q_ref[...]
    lane = lax.broadcasted_iota(I32, q.shape, 1)
    halves = (lane < dh, lane >= dh)
    qs = [jnp.where(hm, q, jnp.zeros_like(q)) for hm in halves]
    def step(kt0, nkt, r0, carry):
        def tiles(load):
            return jnp.concatenate(
                [jnp.concatenate([load(r, c) for c in range(nkt)], axis=1) for r in range(r0, rq)], axis=0)

        start = pl.multiple_of(kt0 * blk, blk)
        k = k_ref[pl.ds(start, nkt * blk), :]
        v = v_ref[pl.ds(start, nkt * blk), :]
        vlane = lax.broadcasted_iota(I32, v.shape, 1)
        madd = tiles(lambda r, c: m_ref[r, kt0 + c]).astype(F32)
        out = []
        for hh in range(2):
            m_all, acc_all = carry[hh]
            m, acc = m_all[r0 * blk:], acc_all[r0 * blk:]
            bias = tiles(lambda r, c: bias_ref[hh, jnp.minimum(nb - (i * rq + r) + (kt0 + c), nb)])
            s = _dot_nt(qs[hh][r0 * blk:], k) + bias + madd
            m_new = jnp.maximum(m, jnp.max(s, axis=1, keepdims=True))
            p = jnp.exp2(s - m_new)
            vh = jnp.where((vlane < dh) if hh == 0 else (vlane >= dh), v, jnp.ones_like(v))
            acc = acc * jnp.exp2(m - m_new) + _dot(p.astype(BF16), vh)
            if r0:
                m_new = jnp.concatenate([m_all[:r0 * blk], m_new], axis=0)
                acc = jnp.concatenate([acc_all[:r0 * blk], acc], axis=0)
            out.append((m_new, acc))
        return tuple(out)

    init = tuple((jnp.full((tq, 1), NEG, F32), jnp.zeros((tq, LANES), F32)) for _ in range(2))
    carry = lax.fori_loop(0, i, lambda kc, c: step(kc * rk, rk, 0, c), init)
    carry = step(i * rk, rk // 2, 0, carry)
    (_, a0), (_, a1) = step(i * rk + rk // 2, rk // 2, rq // 2, carry)
    o_ref[...] = jnp.where(halves[0], a0 / a0[:, dh:dh + 1], a1 / a1[:, 0:1]).astype(o_ref.dtype)


def dsa_attention(h, madd, bias_vecs, B, S, q_col, k_col, v_col, tq=1024, tk=1024):
    blk = Q_BLOCK
    nq = S // blk
    tq, tk = min(tq, S), min(tk, S)
    assert S % tq == 0 and tq == tk and tq % (2 * blk) == 0
    n_pairs = N_HEADS_DSA * HEAD_DIM_DSA // LANES
    assert bias_vecs.shape == (2 * n_pairs, nq + 1, 2 * blk)
    return pl.pallas_call(
        functools.partial(_dsa_kernel, tq=tq, tk=tk, dh=HEAD_DIM_DSA),
        grid=(n_pairs, B, S // tq),
        in_specs=[pl.BlockSpec((None, tq, LANES), lambda p, b, i: (b, i, q_col + p)),
                  pl.BlockSpec((None, S, LANES), lambda p, b, i: (b, 0, k_col + p)),
                  pl.BlockSpec((None, S, LANES), lambda p, b, i: (b, 0, v_col + p)),
                  pl.BlockSpec((None, tq // blk, nq, blk, blk), lambda p, b, i: (b, i, 0, 0, 0)),
                  pl.BlockSpec((2, nq + 1, 2 * blk), lambda p, b, i: (p, 0, 0))],
        out_specs=pl.BlockSpec((None, tq, LANES), lambda p, b, i: (b, i, p)),
        out_shape=jax.ShapeDtypeStruct((B, S, n_pairs * LANES), BF16),
        scratch_shapes=[pltpu.VMEM((2, nq + 1, blk, blk), F32)],
        compiler_params=_cparams(("parallel", "arbitrary", "arbitrary")),
        name="dsa_attention",
    )(h, h, h, madd, bias_vecs)


def _dil_kernel(q_ref, k_ref, v_ref, bias_ref, o_ref, qf, kf, vf, m_s, l_s, acc_s,
                *, blk, chunk, dils, unroll):
    c = pl.program_id(2)
    base = pl.multiple_of(c * chunk, chunk)

    @pl.when(c == 0)
    def _():
        qf[...] = q_ref[...].astype(F32)
        kf[...] = k_ref[...].astype(F32)
        vf[...] = v_ref[...].astype(F32)

    for g, d in enumerate(dils):
        span = blk * d

        def group(it, carry, g=g, d=d, span=span):
            for jj in range(unroll):
                t = it * unroll + jj
                u = t // d
                r = t - u * d
                start = base + u * span + r
                has_prev = start >= span
                prev = jnp.where(has_prev, start - span, start)
                if d == 1:
                    start = pl.multiple_of(start, blk)
                    prev = pl.multiple_of(prev, blk)
                    rows, prows = pl.ds(start, blk), pl.ds(prev, blk)
                    lrows = pl.ds(pl.multiple_of(start - base, blk), blk)
                else:
                    rows, prows = pl.ds(start, blk, stride=d), pl.ds(prev, blk, stride=d)
                    lrows = pl.ds(start - base, blk, stride=d)
                q = qf[rows, :].astype(BF16)
                kk = jnp.concatenate([kf[prows, :], kf[rows, :]], axis=0).astype(BF16)
                vv = jnp.concatenate([vf[prows, :], vf[rows, :]], axis=0).astype(BF16)
                s = _dot_nt(q, kk) + bias_ref[g, jnp.where(has_prev, 0, 1)]
                m_t = jnp.max(s, axis=1, keepdims=True)
                e = jnp.exp2(s - m_t)
                l_t = jnp.sum(e, axis=1, keepdims=True)
                m_s[g, lrows, :] = jnp.broadcast_to(m_t, (blk, LANES))
                l_s[g, lrows, :] = jnp.broadcast_to(l_t, (blk, LANES))
                acc_s[g, lrows, :] = _dot(e.astype(BF16), vv)
            return carry

        lax.fori_loop(0, chunk // (blk * unroll), group, 0)

    m = m_s[0]
    for g in range(1, len(dils)):
        m = jnp.maximum(m, m_s[g])
    num = den = None
    for g in range(len(dils)):
        w = jnp.exp2(m_s[g] - m)
        num = acc_s[g] * w if num is None else num + acc_s[g] * w
        den = l_s[g] * w if den is None else den + l_s[g] * w
    o_ref[...] = (num / den).astype(o_ref.dtype)


def dilated_attention(h, bias_tiles, B, S, q_col, k_col, v_col, n_heads, unroll=16):
    blk = Q_BLOCK
    dils = tuple(d for _, d in DILATED_CONFIGS)
    assert all(w // d == blk for w, d in DILATED_CONFIGS)
    chunk = blk * max(dils)
    assert S % chunk == 0 and (chunk // blk) % unroll == 0
    dh = HEAD_DIM_DIL
    ng = len(dils)
    return pl.pallas_call(
        functools.partial(_dil_kernel, blk=blk, chunk=chunk, dils=dils, unroll=unroll),
        grid=(B, n_heads, S // chunk),
        in_specs=[pl.BlockSpec((None, S, dh), lambda b, hh, c: (b, 0, q_col + hh)),
                  pl.BlockSpec((None, S, dh), lambda b, hh, c: (b, 0, k_col + hh)),
                  pl.BlockSpec((None, S, dh), lambda b, hh, c: (b, 0, v_col + hh)),
                  pl.BlockSpec((None, ng, 2, blk, 2 * blk), lambda b, hh, c: (hh, 0, 0, 0, 0))],
        out_specs=pl.BlockSpec((None, chunk, dh), lambda b, hh, c: (b, c, hh)),
        out_shape=jax.ShapeDtypeStruct((B, S, n_heads * dh), BF16),
        scratch_shapes=[pltpu.VMEM((S, dh), F32), pltpu.VMEM((S, dh), F32), pltpu.VMEM((S, dh), F32),
                        pltpu.VMEM((ng, chunk, LANES), F32), pltpu.VMEM((ng, chunk, LANES), F32),
                        pltpu.VMEM((ng, chunk, dh), F32)],
        compiler_params=_cparams(("parallel", "parallel", "arbitrary")),
        name="dilated_attention",
    )(h, h, h, bias_tiles)


def _rel_bucket(dist):
    max_exact = N_BUCKETS // 2
    d_f = jnp.maximum(dist, 1).astype(F32)
    large = max_exact + (jnp.log(d_f / max_exact) / math.log(BUCKET_MAX_DIST / max_exact)
                         * (N_BUCKETS - max_exact)).astype(I32)
    large = jnp.minimum(large, N_BUCKETS - 1)
    return jnp.where(dist < max_exact, dist, large)


def _bias_by_distance(rel_bias, n):
    return rel_bias.astype(F32)[_rel_bucket(jnp.arange(n, dtype=I32))].T


def _toeplitz_band(g, nblk):
    blk = Q_BLOCK
    period = blk * (nblk + 2)
    m = jnp.arange(period, dtype=I32)
    m = jnp.where(m >= period - blk, m - period, m)
    v = g[:, jnp.clip(blk * nblk - m, 0, g.shape[1] - 1)]
    flat = jnp.tile(v, (1, blk + 1))[:, :blk * (period - 1)]
    return flat.reshape(g.shape[0], blk, period - 1)[:, :, :blk * (nblk + 1)]


def _causal_bias_vectors(rel_bias, nq):
    blk = Q_BLOCK
    g = _bias_by_distance(rel_bias, nq * blk) * LOG2E
    m = np.arange(2 * blk)
    m = np.where(m >= blk, m - 2 * blk, m)
    dist = blk * (nq - np.arange(nq + 1))[:, None] - m[None, :]
    return g[:, np.clip(dist, 0, nq * blk - 1)]


def _dilated_bias_tiles(rel_bias):
    blk = Q_BLOCK
    g = _bias_by_distance(rel_bias, 2 * blk * max(d for _, d in DILATED_CONFIGS)) * LOG2E
    j = blk + np.arange(blk)[:, None] - np.arange(2 * blk)[None, :]
    band = (j >= 0) & (j <= blk)
    masks = np.stack([band, band & (np.arange(2 * blk)[None, :] >= blk)])
    tiles = jnp.stack([_toeplitz_band(g[:, ::d][:, :2 * blk], 1) for _, d in DILATED_CONFIGS], axis=1)
    return jnp.where(masks[None, None], tiles[:, :, None], NEG)


def _lane_slabs(w, width):
    K, N = w.shape
    w = w.reshape(K, N // width, width)
    return jnp.pad(w, ((0, 0), (0, 0), (0, LANES - width))).reshape(K, N // width * LANES)


def even_mixer(x, w_in, idx, rel_bias, B, S):
    half = N_HEADS_SB * HEAD_DIM_SB
    n_main = 6 * half
    n_qi = N_HEADS_IDX * HEAD_DIM_IDX
    slab = half // LANES
    col_scale = _col_scale(n_main, [(0, half, HEAD_DIM_SB ** -0.5 * LOG2E),
                                    (3 * half, 4 * half, HEAD_DIM_DSA ** -0.5 * LOG2E)])
    h = matmul(x, w_in, idx, col_scale, BF16).reshape(B, S, n_main)
    w_idx = w_in[idx, :, n_main:]
    w_idx = jnp.concatenate([_lane_slabs(w_idx[:, :n_qi], HEAD_DIM_IDX),
                             _lane_slabs(w_idx[:, n_qi:n_qi + HEAD_DIM_IDX], HEAD_DIM_IDX),
                             _lane_slabs(w_idx[:, n_qi + HEAD_DIM_IDX:], N_HEADS_IDX)], axis=1)
    idx_scale = _col_scale(w_idx.shape[1], [(0, N_HEADS_IDX * LANES, HEAD_DIM_IDX ** -0.5)])
    hi = matmul_slabs(x, w_idx.astype(BF16), idx_scale, B, S)
    madd = dsa_select(hi, min(TOPK_MAX, S // 4))
    oa = stick_breaking(h, B, S, 0, slab, 2 * slab, N_HEADS_SB)
    ob = dsa_attention(h, madd, _causal_bias_vectors(rel_bias, S // Q_BLOCK), B, S,
                       3 * slab, 4 * slab, 5 * slab)
    return [oa.reshape(B * S, half), ob.reshape(B * S, half)]


def odd_mixer(xb, w_in, idx, rel_bias, B, S):
    width = N_HEADS_DIL * HEAD_DIM_DIL
    col_scale = _col_scale(3 * width, [(0, width, HEAD_DIM_DIL ** -0.5 * LOG2E)])
    h = matmul(xb, w_in, idx, col_scale, BF16).reshape(B, S, 3 * width)
    o = dilated_attention(h, _dilated_bias_tiles(rel_bias), B, S, 0, N_HEADS_DIL, 2 * N_HEADS_DIL,
                          N_HEADS_DIL)
    return [o.reshape(B * S, width)]


def kernel(x, even_w_in, even_w_out, odd_w_in, odd_w_out, rel_bias, ln_mix_g, ln_mix_b,
           ffn_w1, ffn_w2, ln_ffn_g, ln_ffn_b):
    B, S, D = x.shape
    x2 = x.reshape(B * S, D)
    xb = x2
    w_out = (even_w_out.astype(BF16), odd_w_out.astype(BF16))
    w1, w2 = ffn_w1.astype(BF16), ffn_w2.astype(BF16)
    for layer in range(DEPTH):
        idx = layer // 2
        if layer % 2 == 0:
            a_list = even_mixer(xb, even_w_in, idx, rel_bias, B, S)
        else:
            a_list = odd_mixer(xb, odd_w_in, idx, rel_bias, B, S)
        x2, xb = proj_residual_ln(a_list, w_out[layer % 2], idx, x2, ln_mix_g[layer], ln_mix_b[layer])
        x2, xb = ffn_residual_ln(x2, xb, w1, w2, layer, ln_ffn_g[layer], ln_ffn_b[layer])
    return x2.reshape(B, S, D)
```

```python
import functools
import math

import jax
import jax.numpy as jnp
import numpy as np
from jax import lax
from jax.experimental import pallas as pl
from jax.experimental.pallas import tpu as pltpu

F32 = jnp.float32
BF16 = jnp.bfloat16
I32 = jnp.int32

Q_BLOCK = 128
HEAD_DIM_SB = 128
N_HEADS_SB = 8
N_HEADS_DSA = 16
HEAD_DIM_DSA = 64
N_HEADS_IDX = 8
HEAD_DIM_IDX = 64
TOPK_MAX = 256
N_HEADS_DIL = 16
HEAD_DIM_DIL = 128
DILATED_CONFIGS = ((128, 1), (512, 4), (2048, 16))
N_BUCKETS = 32
BUCKET_MAX_DIST = 2048
DEPTH = 2
DN_ALPHA = (2 * DEPTH) ** 0.25
LN_EPS = 1e-5
NEG = -1e30
INT_MIN = -(2 ** 31)
LOG2E = math.log2(math.e)
F32_EXP2_ZERO = -150.0

LANES = 128
VMEM_LIMIT_BYTES = 52 * 1024 * 1024


def _cparams(sem):
    return pltpu.CompilerParams(dimension_semantics=sem, vmem_limit_bytes=VMEM_LIMIT_BYTES)


def _dot(a, b):
    return jnp.dot(a, b, preferred_element_type=F32)


def _dot_nt(a, b):
    return lax.dot_general(a, b, (((1,), (1,)), ((), ())), preferred_element_type=F32)


def _mm_kernel(a_ref, w_ref, s_ref, o_ref, wb_ref):
    @pl.when(pl.program_id(1) == 0)
    def _():
        wb_ref[...] = w_ref[...].astype(BF16)

    acc = _dot(a_ref[...].astype(BF16), wb_ref[...])
    o_ref[...] = (acc * s_ref[...]).astype(o_ref.dtype)


def matmul(a, w, layer, col_scale, out_dtype, tm=1024, tn=1024):
    M, K = a.shape
    N = col_scale.shape[0]
    tm, tn = min(tm, M), min(tn, N)
    assert M % tm == 0 and N % tn == 0 and N <= w.shape[2]
    return pl.pallas_call(
        _mm_kernel,
        grid=(N // tn, M // tm),
        in_specs=[pl.BlockSpec((tm, K), lambda j, i: (i, 0)),
                  pl.BlockSpec((None, K, tn), lambda j, i: (layer, 0, j)),
                  pl.BlockSpec((1, tn), lambda j, i: (0, j))],
        out_specs=pl.BlockSpec((tm, tn), lambda j, i: (i, j)),
        out_shape=jax.ShapeDtypeStruct((M, N), out_dtype),
        scratch_shapes=[pltpu.VMEM((K, tn), BF16)],
        compiler_params=_cparams(("parallel", "arbitrary")),
        name="proj_matmul",
    )(a, w, col_scale.reshape(1, N))


def _col_scale(n, scaled):
    s = np.ones((n,), np.float32)
    for lo, hi, f in scaled:
        s[lo:hi] = f
    return jnp.asarray(s)


def _mm_slab_kernel(a_ref, b_ref, s_ref, o_ref):
    acc = _dot(a_ref[...].astype(BF16), b_ref[...]) * s_ref[...]
    for j in range(o_ref.shape[0]):
        o_ref[j] = acc[:, j * LANES:(j + 1) * LANES]


def matmul_slabs(a, b, col_scale, B, S, tm=1024, tn=1280):
    M, K = a.shape
    N = b.shape[1]
    tm, tn = min(tm, S), min(tn, N)
    assert M == B * S and S % tm == 0 and N % tn == 0 and tn % LANES == 0
    steps = S // tm
    return pl.pallas_call(
        _mm_slab_kernel,
        grid=(B, steps, N // tn),
        in_specs=[pl.BlockSpec((tm, K), lambda b, i, j: (b * steps + i, 0)),
                  pl.BlockSpec((K, tn), lambda b, i, j: (0, j)),
                  pl.BlockSpec((1, tn), lambda b, i, j: (0, j))],
        out_specs=pl.BlockSpec((None, tn // LANES, tm, LANES), lambda b, i, j: (b, j, i, 0)),
        out_shape=jax.ShapeDtypeStruct((B, N // LANES, S, LANES), F32),
        compiler_params=_cparams(("parallel", "parallel", "arbitrary")),
        name="indexer_proj",
    )(a, b, col_scale.reshape(1, N))


def _layer_norm(y, g, b):
    mu = jnp.mean(y, axis=-1, keepdims=True)
    yc = y - mu
    var = jnp.mean(jnp.square(yc), axis=-1, keepdims=True)
    return yc * lax.rsqrt(var + LN_EPS) * g + b


def _proj_ln_kernel(*refs, n_in):
    a_refs, w_refs = refs[:n_in], refs[n_in:2 * n_in]
    x_ref, g_ref, b_ref, y_ref, yb_ref = refs[2 * n_in:]
    acc = _dot(a_refs[0][...], w_refs[0][...])
    for a_ref, w_ref in zip(a_refs[1:], w_refs[1:]):
        acc = acc + _dot(a_ref[...], w_ref[...])
    y = _layer_norm(DN_ALPHA * x_ref[...] + acc, g_ref[...], b_ref[...])
    y_ref[...] = y
    yb_ref[...] = y.astype(BF16)


def proj_residual_ln(a_list, w, layer, x, g, b, tm=512):
    M, D = x.shape
    tm = min(tm, M)
    assert M % tm == 0
    n_in = len(a_list)
    kw = a_list[0].shape[1]
    assert all(a.shape[1] == kw for a in a_list) and n_in * kw == w.shape[1]
    in_specs = ([pl.BlockSpec((tm, kw), lambda i: (i, 0)) for _ in a_list]
                + [pl.BlockSpec((None, kw, D), lambda i, r=r: (layer, r, 0)) for r in range(n_in)]
                + [pl.BlockSpec((tm, D), lambda i: (i, 0)),
                   pl.BlockSpec((1, D), lambda i: (0, 0)),
                   pl.BlockSpec((1, D), lambda i: (0, 0))])
    return pl.pallas_call(
        functools.partial(_proj_ln_kernel, n_in=n_in),
        grid=(M // tm,),
        in_specs=in_specs,
        out_specs=[pl.BlockSpec((tm, D), lambda i: (i, 0)),
                   pl.BlockSpec((tm, D), lambda i: (i, 0))],
        out_shape=[jax.ShapeDtypeStruct((M, D), F32), jax.ShapeDtypeStruct((M, D), BF16)],
        compiler_params=_cparams(("parallel",)),
        name="out_proj_ln",
    )(*a_list, *([w] * n_in), x, g.reshape(1, D), b.reshape(1, D))


def _ffn_kernel(xb_ref, x_ref, w1_ref, w2_ref, g_ref, b_ref, y_ref, yb_ref, acc_ref):
    f = pl.program_id(1)

    @pl.when(f == 0)
    def _():
        acc_ref[...] = jnp.zeros_like(acc_ref)

    h = _dot(xb_ref[...], w1_ref[...])
    h = jnp.square(jnp.maximum(h, 0.0)).astype(BF16)
    acc_ref[...] += _dot(h, w2_ref[...])

    @pl.when(f == pl.num_programs(1) - 1)
    def _():
        y = _layer_norm(DN_ALPHA * x_ref[...] + acc_ref[...], g_ref[...], b_ref[...])
        y_ref[...] = y
        yb_ref[...] = y.astype(BF16)


def ffn_residual_ln(x, xb, w1, w2, layer, g, b, tm=512, tf=1024):
    M, D = x.shape
    F = w1.shape[2]
    tm, tf = min(tm, M), min(tf, F)
    assert M % tm == 0 and F % tf == 0
    return pl.pallas_call(
        _ffn_kernel,
        grid=(M // tm, F // tf),
        in_specs=[pl.BlockSpec((tm, D), lambda i, f: (i, 0)),
                  pl.BlockSpec((tm, D), lambda i, f: (i, 0)),
                  pl.BlockSpec((None, D, tf), lambda i, f: (layer, 0, f)),
                  pl.BlockSpec((None, tf, D), lambda i, f: (layer, f, 0)),
                  pl.BlockSpec((1, D), lambda i, f: (0, 0)),
                  pl.BlockSpec((1, D), lambda i, f: (0, 0))],
        out_specs=[pl.BlockSpec((tm, D), lambda i, f: (i, 0)),
                   pl.BlockSpec((tm, D), lambda i, f: (i, 0))],
        out_shape=[jax.ShapeDtypeStruct((M, D), F32), jax.ShapeDtypeStruct((M, D), BF16)],
        scratch_shapes=[pltpu.VMEM((tm, D), F32)],
        compiler_params=_cparams(("parallel", "arbitrary")),
        name="ffn_ln",
    )(xb, x, w1, w2, g.reshape(1, D), b.reshape(1, D))


def _sb_kernel(q_ref, k_ref, v_ref, o_ref, *, tq, sub):
    i = pl.program_id(2)
    nh = q_ref.shape[1] // LANES
    r = lax.broadcasted_iota(I32, (sub, sub), 0)
    c = lax.broadcasted_iota(I32, (sub, sub), 1)
    tri = jnp.where(r > c, 1.0, 0.0).astype(BF16)

    def chunk(start, row0, state, diag):
        return tuple(head_chunk(hh, start, row0, state[hh][0], state[hh][1], diag) for hh in range(nh))

    def head_chunk(hh, start, row0, carry_all, acc_all, diag):
        cols = slice(hh * LANES, (hh + 1) * LANES)
        carry, acc = carry_all[row0:], acc_all[row0:]
        k = k_ref[pl.ds(start, sub), cols]
        v = v_ref[pl.ds(start, sub), cols]
        z = _dot_nt(q_ref[row0:, cols], k)
        neg_abs = pltpu.bitcast(pltpu.bitcast(z, I32) | INT_MIN, F32)
        softplus = jnp.log(1.0 + jnp.exp2(neg_abs)) * LOG2E
        log_beta = jnp.minimum(z, 0.0) - softplus
        log_keep = log_beta - z
        if diag:
            qpos = i * tq + row0 + lax.broadcasted_iota(I32, z.shape, 0)
            strict = (start + lax.broadcasted_iota(I32, z.shape, 1)) < qpos
            log_keep = jnp.where(strict, log_keep, 0.0)
        later = _dot(log_keep.astype(BF16), tri) + carry
        carry = later[:, 0:1] + log_keep[:, 0:1]
        a = jnp.exp2(log_beta + later)
        if diag:
            a = jnp.where(strict, a, 0.0)
        acc = acc + _dot(a.astype(BF16), v)
        if row0:
            carry = jnp.concatenate([carry_all[:row0], carry], axis=0)
            acc = jnp.concatenate([acc_all[:row0], acc], axis=0)
        return carry, acc

    first = pl.multiple_of(i * tq, tq)
    state = tuple((jnp.zeros((tq, 1), F32), jnp.zeros((tq, LANES), F32)) for _ in range(nh))
    for j in reversed(range(tq // sub)):
        state = chunk(pl.multiple_of(first + j * sub, sub), j * sub, state, True)

    def live(st):
        top = functools.reduce(jnp.maximum, [jnp.max(carry) for carry, _ in st[1]])
        return (st[0] < first // sub) & (top >= F32_EXP2_ZERO)

    def step(st):
        start = pl.multiple_of(first - (st[0] + 1) * sub, sub)
        return st[0] + 1, chunk(start, 0, st[1], False)

    _, state = lax.while_loop(live, step, (jnp.int32(0), state))
    for hh in range(nh):
        o_ref[:, hh * LANES:(hh + 1) * LANES] = state[hh][1].astype(o_ref.dtype)


def stick_breaking(h, B, S, q_col, k_col, v_col, n_heads, tq=512, sub=256, heads_per_step=2):
    hps = heads_per_step
    dh = hps * HEAD_DIM_SB
    tq = min(tq, S)
    sub = min(sub, tq)
    assert S % tq == 0 and tq % sub == 0
    assert n_heads % hps == 0 and q_col % hps == 0 and k_col % hps == 0 and v_col % hps == 0
    q_col, k_col, v_col = q_col // hps, k_col // hps, v_col // hps
    return pl.pallas_call(
        functools.partial(_sb_kernel, tq=tq, sub=sub),
        grid=(B, n_heads // hps, S // tq),
        in_specs=[pl.BlockSpec((None, tq, dh), lambda b, hh, i: (b, i, q_col + hh)),
                  pl.BlockSpec((None, S, dh), lambda b, hh, i: (b, 0, k_col + hh)),
                  pl.BlockSpec((None, S, dh), lambda b, hh, i: (b, 0, v_col + hh))],
        out_specs=pl.BlockSpec((None, tq, dh), lambda b, hh, i: (b, i, hh)),
        out_shape=jax.ShapeDtypeStruct((B, S, n_heads * HEAD_DIM_SB), BF16),
        compiler_params=_cparams(("parallel", "parallel", "arbitrary")),
        name="stick_breaking",
    )(h, h, h)


def _sel_kernel(qi_ref, ki_ref, wi_ref, o_ref, keys_ref, *, blk, nk, ch, topk, w_scale):
    n = pl.program_id(1)
    nh = qi_ref.shape[0]
    cw = ch * blk
    last = n // ch
    qall = qi_ref[...].reshape(nh * blk, LANES).astype(BF16)
    w8 = wi_ref[...].T[:nh, :] * w_scale
    row = lax.broadcasted_iota(I32, (blk, blk), 0)
    col = lax.broadcasted_iota(I32, (blk, blk), 1)
    tri = jnp.where(col < row, 1.0, 0.0).astype(BF16)
    qpos = n * blk + lax.broadcasted_iota(I32, (cw, blk), 1)
    koff = lax.broadcasted_iota(I32, (cw, blk), 0)

    def score_keys(cc, diag):
        start = pl.multiple_of(cc * cw, cw)
        d = _dot_nt(ki_ref[pl.ds(start, cw), :].astype(BF16), qall)
        d = jnp.maximum(d, 0.0)
        sc = d[:, 0:blk] * w8[0:1, :]
        for hh in range(1, nh):
            sc = sc + d[:, hh * blk:(hh + 1) * blk] * w8[hh:hh + 1, :]
        sc = jnp.where(sc == 0.0, 0.0, sc)
        bits = pltpu.bitcast(sc, I32)
        key = jnp.where(bits < 0, bits ^ 0x7FFFFFFF, bits)
        if diag:
            key = jnp.where(start + koff <= qpos, key, INT_MIN)
        keys_ref[pl.ds(cc * ch, ch)] = key.reshape(ch, blk, blk)

    score_keys(last, True)
    lax.fori_loop(0, last, lambda cc, c: (score_keys(cc, False), c)[1], 0)

    def count(pred):
        def body(cc, c):
            kk = keys_ref[pl.ds(cc * ch, ch)]
            for j in range(ch):
                c = c + jnp.where(pred(kk[j]), 1, 0)
            return c
        c = lax.fori_loop(0, last + 1, body, jnp.zeros((blk, blk), I32))
        return jnp.sum(c, axis=0, keepdims=True)

    c0 = count(lambda key: key >= 0)
    theta = jnp.where(c0 >= topk, 0, INT_MIN).astype(I32)

    def bit_step(i, theta):
        cand = theta + lax.shift_left(jnp.int32(1), 30 - i)
        c = count(lambda key: key >= cand)
        return jnp.where(c >= topk, cand, theta)

    theta = lax.fori_loop(0, 31, bit_step, theta)
    need = (topk - count(lambda key: key > theta)).astype(F32)

    def emit(cc, carry, diag):
        kk = keys_ref[pl.ds(cc * ch, ch)]
        for j in range(ch):
            key = kk[j]
            eqf = jnp.where(key == theta, 1.0, 0.0)
            rank = _dot(tri, eqf.astype(BF16)) + carry
            sel = jnp.where(key > theta, 1.0, jnp.where(rank < need, eqf, 0.0))
            if diag:
                kb = cc * ch + j
                sel = jnp.where(kb * blk + row <= n * blk + col, sel, 0.0)
            madd = jnp.where(sel > 0.5, 0.0, -jnp.inf)
            o_ref[cc * ch + j] = madd.T.astype(o_ref.dtype)
            carry = carry + jnp.sum(eqf, axis=0, keepdims=True)
        return carry

    carry = lax.fori_loop(0, last, lambda cc, c: emit(cc, c, False), jnp.zeros((1, blk), F32))
    emit(last, carry, True)

    def fill(kb, c):
        o_ref[kb] = jnp.full((blk, blk), -jnp.inf, o_ref.dtype)
        return c

    lax.fori_loop((last + 1) * ch, nk, fill, 0)


def dsa_select(hi, topk, ch=4):
    B, nslab, S, _ = hi.shape
    nh = N_HEADS_IDX
    assert nslab == nh + 2
    blk = Q_BLOCK
    nq = S // blk
    ch = min(ch, nq)
    assert nq % ch == 0
    return pl.pallas_call(
        functools.partial(_sel_kernel, blk=blk, nk=nq, ch=ch, topk=topk, w_scale=N_HEADS_IDX ** -0.5),
        grid=(B, nq),
        in_specs=[pl.BlockSpec((None, nh, blk, LANES), lambda b, i: (b, 0, i, 0)),
                  pl.BlockSpec((None, None, S, LANES), lambda b, i: (b, nh, 0, 0)),
                  pl.BlockSpec((None, None, blk, LANES), lambda b, i: (b, nh + 1, i, 0))],
        out_specs=pl.BlockSpec((None, None, nq, blk, blk), lambda b, i: (b, i, 0, 0, 0)),
        out_shape=jax.ShapeDtypeStruct((B, nq, nq, blk, blk), BF16),
        scratch_shapes=[pltpu.VMEM((nq, blk, blk), I32)],
        compiler_params=_cparams(("parallel", "arbitrary")),
        name="dsa_select",
    )(hi, hi, hi)


def _dsa_kernel(q_ref, k_ref, v_ref, m_ref, vec_ref, o_ref, bias_ref, *, tq, tk, dh):
    i = pl.program_id(2)
    blk = m_ref.shape[-1]
    rq, rk = tq // blk, tk // blk
    nb = bias_ref.shape[1] - 1

    @pl.when((i == 0) & (pl.program_id(1) == 0))
    def _():
        def expand(kk, c):
            for hh in range(2):
                rows = jnp.broadcast_to(vec_ref[hh, pl.ds(kk, 1), :], (blk, 2 * blk))
                bias_ref[hh, kk] = pltpu.roll(rows, 0, 1, stride=1, stride_axis=0)[:, :blk]
            return c
        lax.fori_loop(0, nb + 1, expand, 0)

    q = q_ref[...]
    lane = lax.broadcasted_iota(I32, q.shape, 1)
    halves = (lane < dh, lane >= dh)
    qs = [jnp.where(hm, q, jnp.zeros_like(q)) for hm in halves]
    def step(kt0, nkt, r0, carry):
        def tiles(load):
            return jnp.concatenate(
                [jnp.concatenate([load(r, c) for c in range(nkt)], axis=1) for r in range(r0, rq)], axis=0)

        start = pl.multiple_of(kt0 * blk, blk)
        k = k_ref[pl.ds(start, nkt * blk), :]
        v = v_ref[pl.ds(start, nkt * blk), :]
        vlane = lax.broadcasted_iota(I32, v.shape, 1)
        madd = tiles(lambda r, c: m_ref[r, kt0 + c]).astype(F32)
        out = []
        for hh in range(2):
            m_all, acc_all = carry[hh]
            m, acc = m_all[r0 * blk:], acc_all[r0 * blk:]
            bias = tiles(lambda r, c: bias_ref[hh, jnp.minimum(nb - (i * rq + r) + (kt0 + c), nb)])
            s = _dot_nt(qs[hh][r0 * blk:], k) + bias + madd
            m_new = jnp.maximum(m, jnp.max(s, axis=1, keepdims=True))
            p = jnp.exp2(s - m_new)
            vh = jnp.where((vlane < dh) if hh == 0 else (vlane >= dh), v, jnp.ones_like(v))
            acc = acc * jnp.exp2(m - m_new) + _dot(p.astype(BF16), vh)
            if r0:
                m_new = jnp.concatenate([m_all[:r0 * blk], m_new], axis=0)
                acc = jnp.concatenate([acc_all[:r0 * blk], acc], axis=0)
            out.append((m_new, acc))
        return tuple(out)

    init = tuple((jnp.full((tq, 1), NEG, F32), jnp.zeros((tq, LANES), F32)) for _ in range(2))
    carry = lax.fori_loop(0, i, lambda kc, c: step(kc * rk, rk, 0, c), init)
    carry = step(i * rk, rk // 2, 0, carry)
    (_, a0), (_, a1) = step(i * rk + rk // 2, rk // 2, rq // 2, carry)
    o_ref[...] = jnp.where(halves[0], a0 / a0[:, dh:dh + 1], a1 / a1[:, 0:1]).astype(o_ref.dtype)


def dsa_attention(h, madd, bias_vecs, B, S, q_col, k_col, v_col, tq=1024, tk=1024):
    blk = Q_BLOCK
    nq = S // blk
    tq, tk = min(tq, S), min(tk, S)
    assert S % tq == 0 and tq == tk and tq % (2 * blk) == 0
    n_pairs = N_HEADS_DSA * HEAD_DIM_DSA // LANES
    assert bias_vecs.shape == (2 * n_pairs, nq + 1, 2 * blk)
    return pl.pallas_call(
        functools.partial(_dsa_kernel, tq=tq, tk=tk, dh=HEAD_DIM_DSA),
        grid=(n_pairs, B, S // tq),
        in_specs=[pl.BlockSpec((None, tq, LANES), lambda p, b, i: (b, i, q_col + p)),
                  pl.BlockSpec((None, S, LANES), lambda p, b, i: (b, 0, k_col + p)),
                  pl.BlockSpec((None, S, LANES), lambda p, b, i: (b, 0, v_col + p)),
                  pl.BlockSpec((None, tq // blk, nq, blk, blk), lambda p, b, i: (b, i, 0, 0, 0)),
                  pl.BlockSpec((2, nq + 1, 2 * blk), lambda p, b, i: (p, 0, 0))],
        out_specs=pl.BlockSpec((None, tq, LANES), lambda p, b, i: (b, i, p)),
        out_shape=jax.ShapeDtypeStruct((B, S, n_pairs * LANES), BF16),
        scratch_shapes=[pltpu.VMEM((2, nq + 1, blk, blk), F32)],
        compiler_params=_cparams(("parallel", "arbitrary", "arbitrary")),
        name="dsa_attention",
    )(h, h, h, madd, bias_vecs)


def _dil_kernel(q_ref, k_ref, v_ref, bias_ref, o_ref, qf, kf, vf, m_s, l_s, acc_s,
                *, blk, chunk, dils, unroll):
    c = pl.program_id(2)
    base = pl.multiple_of(c * chunk, chunk)

    @pl.when(c == 0)
    def _():
        qf[...] = q_ref[...].astype(F32)
        kf[...] = k_ref[...].astype(F32)
        vf[...] = v_ref[...].astype(F32)

    for g, d in enumerate(dils):
        span = blk * d

        def group(it, carry, g=g, d=d, span=span):
            for jj in range(unroll):
                t = it * unroll + jj
                u = t // d
                r = t - u * d
                start = base + u * span + r
                has_prev = start >= span
                prev = jnp.where(has_prev, start - span, start)
                if d == 1:
                    start = pl.multiple_of(start, blk)
                    prev = pl.multiple_of(prev, blk)
                    rows, prows = pl.ds(start, blk), pl.ds(prev, blk)
                    lrows = pl.ds(pl.multiple_of(start - base, blk), blk)
                else:
                    rows, prows = pl.ds(start, blk, stride=d), pl.ds(prev, blk, stride=d)
                    lrows = pl.ds(start - base, blk, stride=d)
                q = qf[rows, :].astype(BF16)
                kk = jnp.concatenate([kf[prows, :], kf[rows, :]], axis=0).astype(BF16)
                vv = jnp.concatenate([vf[prows, :], vf[rows, :]], axis=0).astype(BF16)
                s = _dot_nt(q, kk) + bias_ref[g, jnp.where(has_prev, 0, 1)]
                m_t = jnp.max(s, axis=1, keepdims=True)
                e = jnp.exp2(s - m_t)
                l_t = jnp.sum(e, axis=1, keepdims=True)
                m_s[g, lrows, :] = jnp.broadcast_to(m_t, (blk, LANES))
                l_s[g, lrows, :] = jnp.broadcast_to(l_t, (blk, LANES))
                acc_s[g, lrows, :] = _dot(e.astype(BF16), vv)
            return carry

        lax.fori_loop(0, chunk // (blk * unroll), group, 0)

    m = m_s[0]
    for g in range(1, len(dils)):
        m = jnp.maximum(m, m_s[g])
    num = den = None
    for g in range(len(dils)):
        w = jnp.exp2(m_s[g] - m)
        num = acc_s[g] * w if num is None else num + acc_s[g] * w
        den = l_s[g] * w if den is None else den + l_s[g] * w
    o_ref[...] = (num / den).astype(o_ref.dtype)


def dilated_attention(h, bias_tiles, B, S, q_col, k_col, v_col, n_heads, unroll=16):
    blk = Q_BLOCK
    dils = tuple(d for _, d in DILATED_CONFIGS)
    assert all(w // d == blk for w, d in DILATED_CONFIGS)
    chunk = blk * max(dils)
    assert S % chunk == 0 and (chunk // blk) % unroll == 0
    dh = HEAD_DIM_DIL
    ng = len(dils)
    return pl.pallas_call(
        functools.partial(_dil_kernel, blk=blk, chunk=chunk, dils=dils, unroll=unroll),
        grid=(B, n_heads, S // chunk),
        in_specs=[pl.BlockSpec((None, S, dh), lambda b, hh, c: (b, 0, q_col + hh)),
                  pl.BlockSpec((None, S, dh), lambda b, hh, c: (b, 0, k_col + hh)),
                  pl.BlockSpec((None, S, dh), lambda b, hh, c: (b, 0, v_col + hh)),
                  pl.BlockSpec((None, ng, 2, blk, 2 * blk), lambda b, hh, c: (hh, 0, 0, 0, 0))],
        out_specs=pl.BlockSpec((None, chunk, dh), lambda b, hh, c: (b, c, hh)),
        out_shape=jax.ShapeDtypeStruct((B, S, n_heads * dh), BF16),
        scratch_shapes=[pltpu.VMEM((S, dh), F32), pltpu.VMEM((S, dh), F32), pltpu.VMEM((S, dh), F32),
                        pltpu.VMEM((ng, chunk, LANES), F32), pltpu.VMEM((ng, chunk, LANES), F32),
                        pltpu.VMEM((ng, chunk, dh), F32)],
        compiler_params=_cparams(("parallel", "parallel", "arbitrary")),
        name="dilated_attention",
    )(h, h, h, bias_tiles)


def _rel_bucket(dist):
    max_exact = N_BUCKETS // 2
    d_f = jnp.maximum(dist, 1).astype(F32)
    large = max_exact + (jnp.log(d_f / max_exact) / math.log(BUCKET_MAX_DIST / max_exact)
                         * (N_BUCKETS - max_exact)).astype(I32)
    large = jnp.minimum(large, N_BUCKETS - 1)
    return jnp.where(dist < max_exact, dist, large)


def _bias_by_distance(rel_bias, n):
    return rel_bias.astype(F32)[_rel_bucket(jnp.arange(n, dtype=I32))].T


def _toeplitz_band(g, nblk):
    blk = Q_BLOCK
    period = blk * (nblk + 2)
    m = jnp.arange(period, dtype=I32)
    m = jnp.where(m >= period - blk, m - period, m)
    v = g[:, jnp.clip(blk * nblk - m, 0, g.shape[1] - 1)]
    flat = jnp.tile(v, (1, blk + 1))[:, :blk * (period - 1)]
    return flat.reshape(g.shape[0], blk, period - 1)[:, :, :blk * (nblk + 1)]


def _causal_bias_vectors(rel_bias, nq):
    blk = Q_BLOCK
    g = _bias_by_distance(rel_bias, nq * blk) * LOG2E
    m = np.arange(2 * blk)
    m = np.where(m >= blk, m - 2 * blk, m)
    dist = blk * (nq - np.arange(nq + 1))[:, None] - m[None, :]
    return g[:, np.clip(dist, 0, nq * blk - 1)]


def _dilated_bias_tiles(rel_bias):
    blk = Q_BLOCK
    g = _bias_by_distance(rel_bias, 2 * blk * max(d for _, d in DILATED_CONFIGS)) * LOG2E
    j = blk + np.arange(blk)[:, None] - np.arange(2 * blk)[None, :]
    band = (j >= 0) & (j <= blk)
    masks = np.stack([band, band & (np.arange(2 * blk)[None, :] >= blk)])
    tiles = jnp.stack([_toeplitz_band(g[:, ::d][:, :2 * blk], 1) for _, d in DILATED_CONFIGS], axis=1)
    return jnp.where(masks[None, None], tiles[:, :, None], NEG)


def _lane_slabs(w, width):
    K, N = w.shape
    w = w.reshape(K, N // width, width)
    return jnp.pad(w, ((0, 0), (0, 0), (0, LANES - width))).reshape(K, N // width * LANES)


def even_mixer(x, w_in, idx, rel_bias, B, S):
    half = N_HEADS_SB * HEAD_DIM_SB
    n_main = 6 * half
    n_qi = N_HEADS_IDX * HEAD_DIM_IDX
    slab = half // LANES
    col_scale = _col_scale(n_main, [(0, half, HEAD_DIM_SB ** -0.5 * LOG2E),
                                    (3 * half, 4 * half, HEAD_DIM_DSA ** -0.5 * LOG2E)])
    h = matmul(x, w_in, idx, col_scale, BF16).reshape(B, S, n_main)
    w_idx = w_in[idx, :, n_main:]
    w_idx = jnp.concatenate([_lane_slabs(w_idx[:, :n_qi], HEAD_DIM_IDX),
                             _lane_slabs(w_idx[:, n_qi:n_qi + HEAD_DIM_IDX], HEAD_DIM_IDX),
                             _lane_slabs(w_idx[:, n_qi + HEAD_DIM_IDX:], N_HEADS_IDX)], axis=1)
    idx_scale = _col_scale(w_idx.shape[1], [(0, N_HEADS_IDX * LANES, HEAD_DIM_IDX ** -0.5)])
    hi = matmul_slabs(x, w_idx.astype(BF16), idx_scale, B, S)
    madd = dsa_select(hi, min(TOPK_MAX, S // 4))
    oa = stick_breaking(h, B, S, 0, slab, 2 * slab, N_HEADS_SB)
    ob = dsa_attention(h, madd, _causal_bias_vectors(rel_bias, S // Q_BLOCK), B, S,
                       3 * slab, 4 * slab, 5 * slab)
    return [oa.reshape(B * S, half), ob.reshape(B * S, half)]


def odd_mixer(xb, w_in, idx, rel_bias, B, S):
    width = N_HEADS_DIL * HEAD_DIM_DIL
    col_scale = _col_scale(3 * width, [(0, width, HEAD_DIM_DIL ** -0.5 * LOG2E)])
    h = matmul(xb, w_in, idx, col_scale, BF16).reshape(B, S, 3 * width)
    o = dilated_attention(h, _dilated_bias_tiles(rel_bias), B, S, 0, N_HEADS_DIL, 2 * N_HEADS_DIL,
                          N_HEADS_DIL)
    return [o.reshape(B * S, width)]


def kernel(x, even_w_in, even_w_out, odd_w_in, odd_w_out, rel_bias, ln_mix_g, ln_mix_b,
           ffn_w1, ffn_w2, ln_ffn_g, ln_ffn_b):
    B, S, D = x.shape
    x2 = x.reshape(B * S, D)
    xb = x2
    w_out = (even_w_out.astype(BF16), odd_w_out.astype(BF16))
    w1, w2 = ffn_w1.astype(BF16), ffn_w2.astype(BF16)
    for layer in range(DEPTH):
        idx = layer // 2
        if layer % 2 == 0:
            a_list = even_mixer(xb, even_w_in, idx, rel_bias, B, S)
        else:
            a_list = odd_mixer(xb, odd_w_in, idx, rel_bias, B, S)
        x2, xb = proj_residual_ln(a_list, w_out[layer % 2], idx, x2, ln_mix_g[layer], ln_mix_b[layer])
        x2, xb = ffn_residual_ln(x2, xb, w1, w2, layer, ln_ffn_g[layer], ln_ffn_b[layer])
    return x2.reshape(B, S, D)
```
